```python
import math
import jax
import jax.numpy as jnp
from jax import lax
import numpy as np

D_MODEL = 1024
BATCH = 16
SEQ = 256
DEPTH = 4
DEC_BATCH = 4
DEC_SEQ = 2048
PAST_LEN = 512

GRID_W = 64
N_ATTN_LAYERS = (DEPTH + 1) // 2
N_HYENA_LAYERS = DEPTH // 2
MLA_HEADS = 8
Q_RANK = 384
KV_RANK = 128
NOPE_DIM = 64
ROPE_DIM = 32
V_DIM = 64
QK_DIM = NOPE_DIM + ROPE_DIM
ROPE_THETA = 10000.0
NA_HEADS = 8
NA_HEAD_DIM = 64
NA_KH = 8
NA_KW = 16
NA_KBW = 2 * NA_KW
NA_WIDTH = NA_HEADS * NA_HEAD_DIM
ATTN_IN = Q_RANK + KV_RANK + ROPE_DIM + 3 * NA_WIDTH
ATTN_OUT = MLA_HEADS * V_DIM + NA_WIDTH
Q_BLOCK = 128
HY_ORDER = 2
HY_BANDS = 8
HY_EMB = 1 + 2 * HY_BANDS
HY_FF = 64
HY_SHORT = 3
HY_TARGET = 1e-2
HY_FAST_DECAY = 0.3
HY_SLOW_DECAY = 1.5
HY_MOD_SHIFT = 0.05
N_GROUPS = 4
EXPERTS_PER_GROUP = 8
N_EXPERTS = N_GROUPS * EXPERTS_PER_GROUP
TOP_K = 2
D_EXPERT = 256
EPS = 1e-6
NEG_INF = -1e30

kernel_name = 'hybrid_mla_natten_hyena_hmoe_step'


def rmsnorm(x, g):
    x32 = x.astype(jnp.float32)
    y = x32 * lax.rsqrt(jnp.mean(x32 * x32, axis=-1, keepdims=True) + EPS)
    return (y * g.astype(jnp.float32)).astype(x.dtype)


def adaln(cvec, w, b):
    m = jax.nn.silu(cvec) @ w + b
    return [t[:, None, :] for t in jnp.split(m, 6, axis=-1)]


def blocked_attention(q, k, v, scale):
    B, Lq, H, dk = q.shape
    nblk = Lq // Q_BLOCK
    qb = jnp.moveaxis(q.reshape(B, nblk, Q_BLOCK, H, dk), 1, 0)

    def one_block(qblk):
        s = jnp.einsum('bqhd,bkhd->bhqk', qblk, k).astype(jnp.float32) * scale
        p = jax.nn.softmax(s, axis=-1).astype(v.dtype)
        return jnp.einsum('bhqk,bkhd->bqhd', p, v)

    o = lax.map(one_block, qb)
    return jnp.moveaxis(o, 0, 1).reshape(B, Lq, H, v.shape[-1])


def axial_rope(x):
    L = x.shape[1]
    t = jnp.arange(L)
    quarter = ROPE_DIM // 4
    inv = ROPE_THETA ** (-jnp.arange(quarter, dtype=jnp.float32) / quarter)

    def rot(xp, pos):
        ang = pos.astype(jnp.float32)[:, None] * inv[None, :]
        cos = jnp.cos(ang)[None, :, None, :]
        sin = jnp.sin(ang)[None, :, None, :]
        x1, x2 = jnp.split(xp.astype(jnp.float32), 2, axis=-1)
        return jnp.concatenate([x1 * cos - x2 * sin, x2 * cos + x1 * sin], axis=-1)

    x_nope, x_row, x_col = jnp.split(x, [NOPE_DIM, NOPE_DIM + ROPE_DIM // 2], axis=-1)
    out = jnp.concatenate([x_nope.astype(jnp.float32), rot(x_row, t // GRID_W), rot(x_col, t % GRID_W)], axis=-1)
    return out.astype(x.dtype)


def mla_expand(ckv, k_rope, w_kv_up, k_gain):
    B, L, _ = ckv.shape
    kv = (ckv @ w_kv_up).reshape(B, L, MLA_HEADS, NOPE_DIM + V_DIM)
    k_nope, v = jnp.split(kv, [NOPE_DIM], axis=-1)
    k_pe = jnp.broadcast_to(k_rope[:, :, None, :], (B, L, MLA_HEADS, ROPE_DIM))
    k = rmsnorm(jnp.concatenate([k_nope, k_pe], axis=-1), k_gain)
    return k, v


def attn_in_proj(h, w_in, q_norm, kv_norm, w_q_up, w_kv_up, q_gain, k_gain, na_q_gain, na_k_gain):
    B, L, _ = h.shape
    p = h @ w_in
    q_lat, kv_lat, k_rope, na_qkv = jnp.split(p, [Q_RANK, Q_RANK + KV_RANK, Q_RANK + KV_RANK + ROPE_DIM], axis=-1)
    q = rmsnorm((rmsnorm(q_lat, q_norm) @ w_q_up).reshape(B, L, MLA_HEADS, QK_DIM), q_gain)
    ckv = rmsnorm(kv_lat, kv_norm)
    k, v = mla_expand(ckv, k_rope, w_kv_up, k_gain)
    na_qkv = na_qkv.reshape(B, L, 3, NA_HEADS, NA_HEAD_DIM)
    nq = rmsnorm(na_qkv[:, :, 0], na_q_gain)
    nk = rmsnorm(na_qkv[:, :, 1], na_k_gain)
    nv = na_qkv[:, :, 2]
    return q, k, v, ckv, k_rope, nq, nk, nv


def na_layout(rows):
    wh = min(NA_KH, rows)
    ncb = GRID_W // NA_KW
    r = np.arange(rows)
    rs = np.clip(r - wh // 2, 0, rows - wh)
    key_rows = rs[:, None] + np.arange(wh)[None, :]
    kb = np.clip(np.arange(ncb) * NA_KW - NA_KW // 2, 0, GRID_W - NA_KBW)
    key_cols = kb[:, None] + np.arange(NA_KBW)[None, :]
    key_idx = key_rows[:, None, :, None] * GRID_W + key_cols[None, :, None, :]
    q_cols = np.arange(ncb)[:, None] * NA_KW + np.arange(NA_KW)[None, :]
    cs = np.clip(q_cols - NA_KW // 2, 0, GRID_W - NA_KW)
    col_ok = (key_cols[:, None, :] >= cs[:, :, None]) & (key_cols[:, None, :] < cs[:, :, None] + NA_KW)
    valid = np.broadcast_to(col_ok[:, :, None, :], (ncb, NA_KW, wh, NA_KBW)).reshape(ncb, NA_KW, wh * NA_KBW)
    dy = key_rows - r[:, None] + NA_KH - 1
    dx = np.clip(key_cols[:, None, :] - q_cols[:, :, None], -(NA_KW - 1), NA_KW - 1) + NA_KW - 1
    return key_idx.reshape(rows, ncb, wh * NA_KBW), valid, dy, dx, wh


def neighbourhood_attention(q, k, v, k_ctx, v_ctx, rpb):
    B, L, H, d = q.shape
    rows = L // GRID_W
    ncb = GRID_W // NA_KW
    key_idx, valid, dy, dx, wh = na_layout(rows)
    nk = wh * NA_KBW
    scale = d ** -0.5
    flat_idx = jnp.asarray(key_idx.reshape(-1))
    qb = q.reshape(B, rows, ncb, NA_KW, H, d)
    kg = jnp.take(k, flat_idx, axis=1).reshape(B, rows, ncb, nk, H, d)
    vg = jnp.take(v, flat_idx, axis=1).reshape(B, rows, ncb, nk, H, d)
    bias = rpb[:, dy[:, None, None, :, None], dx[None, :, :, None, :]].reshape(H, rows, ncb, NA_KW, nk)
    bias = jnp.transpose(bias, (1, 2, 0, 3, 4)).astype(jnp.float32)
    s_win = jnp.einsum('brjqhd,brjkhd->brjhqk', qb, kg).astype(jnp.float32) * scale + bias[None]
    s_win = jnp.where(jnp.asarray(valid)[None, None, :, None], s_win, NEG_INF)
    s_ctx = jnp.einsum('brjqhd,bchd->brjhqc', qb, k_ctx).astype(jnp.float32) * scale
    p = jax.nn.softmax(jnp.concatenate([s_win, s_ctx], axis=-1), axis=-1).astype(v.dtype)
    o = (jnp.einsum('brjhqk,brjkhd->brjqhd', p[..., :nk], vg)
         + jnp.einsum('brjhqc,bchd->brjqhd', p[..., nk:], v_ctx))
    return o.reshape(B, L, H, d)


def hyena_filter_spectrum(L, w1, b1, freq, w2, b2, w3):
    f32 = jnp.float32
    t = jnp.linspace(0.0, 1.0, L, dtype=f32)[:, None]
    w = 2.0 * math.pi * jnp.arange(L, dtype=f32)[:, None] / L
    bands = jnp.linspace(1e-4, HY_BANDS - 1, HY_BANDS, dtype=f32)[None, :]
    z = jnp.concatenate([t, jnp.cos(bands * w), -jnp.sin(bands * w)], axis=-1)
    h = jnp.sin(freq[0].astype(f32) * (z @ w1.astype(f32) + b1.astype(f32)))
    h = jnp.sin(freq[1].astype(f32) * (h @ w2.astype(f32) + b2.astype(f32)))
    h = (h @ w3.astype(f32)).reshape(L, 2, HY_ORDER, D_MODEL)
    deltas = jnp.abs(jnp.linspace(math.log(HY_TARGET) / HY_FAST_DECAY, math.log(HY_TARGET) / HY_SLOW_DECAY,
                                  D_MODEL, dtype=f32))
    h = h * (jnp.exp(-t * deltas[None, :]) + HY_MOD_SHIFT)[:, None, None, :]
    h_fwd, h_bwd = h[:, 0], h[:, 1]
    k_full = jnp.concatenate([h_fwd, jnp.zeros((1, HY_ORDER, D_MODEL), f32), h_bwd[1:][::-1]], axis=0)
    k_full = k_full / (jnp.sum(jnp.abs(k_full), axis=0, keepdims=True) + EPS)
    return jnp.fft.rfft(k_full, n=2 * L, axis=0)


def fft_longconv(u, kf, skip):
    L = u.shape[1]
    u32 = u.astype(jnp.float32)
    y = jnp.fft.irfft(jnp.fft.rfft(u32, n=2 * L, axis=1) * kf[None], n=2 * L, axis=1)[:, :L]
    return (y + u32 * skip.astype(jnp.float32)).astype(u.dtype)


def hyena_mixer(h, w_in, b_in, conv_w, conv_b, fw1, fb1, ffreq, fw2, fb2, fw3, skip, w_out):
    B, L, _ = h.shape
    p = h @ w_in + b_in
    p = lax.conv_general_dilated(p, conv_w.astype(p.dtype)[:, None, :], window_strides=(1,),
                                 padding=[(HY_SHORT // 2, HY_SHORT // 2)],
                                 dimension_numbers=('NWC', 'WIO', 'NWC'),
                                 feature_group_count=3 * D_MODEL) + conv_b
    v, x1, x2 = jnp.split(p, 3, axis=-1)
    kf = hyena_filter_spectrum(L, fw1, fb1, ffreq, fw2, fb2, fw3)
    z = x1 * fft_longconv(v, kf[:, 0], skip[0])
    z = x2 * fft_longconv(z, kf[:, 1], skip[1])
    return z @ w_out


def hier_moe(h, w_grp, b_grp, w_rt, b_rt, w_gate, w_up, w_down):
    B, L, D = h.shape
    t = h.reshape(B * L, D)
    n = t.shape[0]
    g_prob = jax.nn.softmax((t @ w_grp + b_grp).astype(jnp.float32), axis=-1)
    g_w, g_i = lax.top_k(g_prob, 1)
    e_logits = (t @ w_rt + b_rt).astype(jnp.float32).reshape(n, N_GROUPS, EXPERTS_PER_GROUP)
    e_logits = e_logits[jnp.arange(n), g_i[:, 0]]
    e_w, e_i = lax.top_k(jax.nn.softmax(e_logits, axis=-1), TOP_K)
    weights = e_w / jnp.sum(e_w, axis=-1, keepdims=True) * g_w
    ids = g_i * EXPERTS_PER_GROUP + e_i
    gates = jnp.sum(jax.nn.one_hot(ids, N_EXPERTS, dtype=jnp.float32) * weights[..., None], axis=1).astype(h.dtype)
    hg = jnp.einsum('nd,edf->nef', t, w_gate)
    hu = jnp.einsum('nd,edf->nef', t, w_up)
    y = jnp.einsum('nef,efd->nd', jax.nn.silu(hg) * hu * gates[:, :, None], w_down)
    return y.reshape(B, L, D)


def setup_inputs(seed: int = 0) -> dict:
    key = jax.random.key(seed)
    ks = jax.random.split(key, 64)
    counter = [0]

    def nrm(shape, scale=1.0):
        k = ks[counter[0]]
        counter[0] += 1
        return jax.random.normal(k, shape, jnp.float32) * scale

    def gain(shape):
        return 1.0 + nrm(shape, 0.05)

    D = D_MODEL
    NA = N_ATTN_LAYERS
    NH = N_HYENA_LAYERS
    return {
        'x_prompt': nrm((BATCH, SEQ, D)),
        'x_sample': nrm((DEC_BATCH, DEC_SEQ, D)),
        'cache_mla_ckv': nrm((DEC_BATCH, NA, PAST_LEN, KV_RANK)),
        'cache_mla_krope': nrm((DEC_BATCH, NA, PAST_LEN, ROPE_DIM)),
        'cache_na_k': nrm((DEC_BATCH, NA, PAST_LEN, NA_HEADS, NA_HEAD_DIM)),
        'cache_na_v': nrm((DEC_BATCH, NA, PAST_LEN, NA_HEADS, NA_HEAD_DIM)),
        'c': nrm((DEC_BATCH, D)),
        'c_ctx': nrm((D,)),
        'ada_w': nrm((DEPTH, D, 6 * D), 0.5 * D ** -0.5),
        'ada_b': nrm((DEPTH, 6 * D), 0.02),
        'norm_mix': gain((DEPTH, D)),
        'norm_ffn': gain((DEPTH, D)),
        'attn_w_in': nrm((NA, D, ATTN_IN), D ** -0.5),
        'mla_q_norm': gain((NA, Q_RANK)),
        'mla_kv_norm': gain((NA, KV_RANK)),
        'mla_w_q_up': nrm((NA, Q_RANK, MLA_HEADS * QK_DIM), Q_RANK ** -0.5),
        'mla_w_kv_up': nrm((NA, KV_RANK, MLA_HEADS * (NOPE_DIM + V_DIM)), KV_RANK ** -0.5),
        'mla_q_gain': gain((NA, QK_DIM)),
        'mla_k_gain': gain((NA, QK_DIM)),
        'na_q_gain': gain((NA, NA_HEAD_DIM)),
        'na_k_gain': gain((NA, NA_HEAD_DIM)),
        'na_rpb': nrm((NA, NA_HEADS, 2 * NA_KH - 1, 2 * NA_KW - 1), 0.5),
        'attn_w_out': nrm((NA, ATTN_OUT, D), ATTN_OUT ** -0.5),
        'hy_w_in': nrm((NH, D, 3 * D), D ** -0.5),
        'hy_b_in': nrm((NH, 3 * D), 0.02),
        'hy_conv_w': nrm((NH, HY_SHORT, 3 * D), HY_SHORT ** -0.5),
        'hy_conv_b': nrm((NH, 3 * D), 0.02),
        'hy_filt_w1': nrm((NH, HY_EMB, HY_FF), HY_EMB ** -0.5),
        'hy_filt_b1': nrm((NH, HY_FF), 0.1),
        'hy_filt_freq': 1.0 + nrm((NH, 2, HY_FF), 0.1),
        'hy_filt_w2': nrm((NH, HY_FF, HY_FF), HY_FF ** -0.5),
        'hy_filt_b2': nrm((NH, HY_FF), 0.1),
        'hy_filt_w3': nrm((NH, HY_FF, 2 * HY_ORDER * D), HY_FF ** -0.5),
        'hy_skip': nrm((NH, HY_ORDER, D), 0.5),
        'hy_w_out': nrm((NH, D, D), D ** -0.5),
        'moe_w_group': nrm((DEPTH, D, N_GROUPS), D ** -0.5),
        'moe_b_group': nrm((DEPTH, N_GROUPS), 0.01),
        'moe_w_router': nrm((DEPTH, D, N_EXPERTS), D ** -0.5),
        'moe_b_router': nrm((DEPTH, N_EXPERTS), 0.01),
        'moe_w_gate': nrm((DEPTH, N_EXPERTS, D, D_EXPERT), D ** -0.5),
        'moe_w_up': nrm((DEPTH, N_EXPERTS, D, D_EXPERT), D ** -0.5),
        'moe_w_down': nrm((DEPTH, N_EXPERTS, D_EXPERT, D), D_EXPERT ** -0.5),
    }


def reference(x_prompt, x_sample, cache_mla_ckv, cache_mla_krope, cache_na_k, cache_na_v, c, c_ctx,
              ada_w, ada_b, norm_mix, norm_ffn,
              attn_w_in, mla_q_norm, mla_kv_norm, mla_w_q_up, mla_w_kv_up, mla_q_gain, mla_k_gain,
              na_q_gain, na_k_gain, na_rpb, attn_w_out,
              hy_w_in, hy_b_in, hy_conv_w, hy_conv_b, hy_filt_w1, hy_filt_b1, hy_filt_freq,
              hy_filt_w2, hy_filt_b2, hy_filt_w3, hy_skip, hy_w_out,
              moe_w_group, moe_b_group, moe_w_router, moe_b_router, moe_w_gate, moe_w_up, moe_w_down):
    mla_scale = QK_DIM ** -0.5
    na_scale = NA_HEAD_DIM ** -0.5

    def attn_params(a):
        return (attn_w_in[a], mla_q_norm[a], mla_kv_norm[a], mla_w_q_up[a], mla_w_kv_up[a],
                mla_q_gain[a], mla_k_gain[a], na_q_gain[a], na_k_gain[a])

    def hyena_params(j):
        return (hy_w_in[j], hy_b_in[j], hy_conv_w[j], hy_conv_b[j], hy_filt_w1[j], hy_filt_b1[j],
                hy_filt_freq[j], hy_filt_w2[j], hy_filt_b2[j], hy_filt_w3[j], hy_skip[j], hy_w_out[j])

    def merge_heads(o_mla, o_na, a):
        B, L = o_mla.shape[:2]
        return jnp.concatenate([o_mla.reshape(B, L, -1), o_na.reshape(B, L, -1)], axis=-1) @ attn_w_out[a]

    def channel_mixer(x, l, shift, scale, gate):
        h = rmsnorm(x, norm_ffn[l]) * (1 + scale) + shift
        return x + gate * hier_moe(h, moe_w_group[l], moe_b_group[l], moe_w_router[l], moe_b_router[l],
                                   moe_w_gate[l], moe_w_up[l], moe_w_down[l])

    xp = x_prompt
    ckv_list, krope_list, nk_list, nv_list = [], [], [], []
    for l in range(DEPTH):
        sh1, sc1, g1, sh2, sc2, g2 = adaln(c_ctx[None, :], ada_w[l], ada_b[l])
        h = rmsnorm(xp, norm_mix[l]) * (1 + sc1) + sh1
        if l % 2 == 0:
            a = l // 2
            q, k, v, ckv, krope, nq, nk, nv = attn_in_proj(h, *attn_params(a))
            out = merge_heads(blocked_attention(q, k, v, mla_scale), blocked_attention(nq, nk, nv, na_scale), a)
            ckv_list.append(ckv)
            krope_list.append(krope)
            nk_list.append(nk)
            nv_list.append(nv)
        else:
            out = hyena_mixer(h, *hyena_params(l // 2))
        xp = channel_mixer(xp + g1 * out, l, sh2, sc2, g2)
    y_prompt = xp
    new_mla_ckv = jnp.stack(ckv_list, axis=1)
    new_mla_krope = jnp.stack(krope_list, axis=1)
    new_na_k = jnp.stack(nk_list, axis=1)
    new_na_v = jnp.stack(nv_list, axis=1)

    xs = x_sample
    for l in range(DEPTH):
        sh1, sc1, g1, sh2, sc2, g2 = adaln(c, ada_w[l], ada_b[l])
        h = rmsnorm(xs, norm_mix[l]) * (1 + sc1) + sh1
        if l % 2 == 0:
            a = l // 2
            q, k, v, _, _, nq, nk, nv = attn_in_proj(h, *attn_params(a))
            q = axial_rope(q)
            k = axial_rope(k)
            k_ctx, v_ctx = mla_expand(cache_mla_ckv[:, a], cache_mla_krope[:, a], mla_w_kv_up[a], mla_k_gain[a])
            o_mla = blocked_attention(q, jnp.concatenate([k, k_ctx], axis=1),
                                      jnp.concatenate([v, v_ctx], axis=1), mla_scale)
            o_na = neighbourhood_attention(nq, nk, nv, cache_na_k[:, a], cache_na_v[:, a], na_rpb[a])
            out = merge_heads(o_mla, o_na, a)
        else:
            out = hyena_mixer(h, *hyena_params(l // 2))
        xs = channel_mixer(xs + g1 * out, l, sh2, sc2, g2)
    y_sample = xs

    return (y_prompt, y_sample, new_mla_ckv, new_mla_krope, new_na_k, new_na_v)
```

```python
import functools
import math

import jax
import jax.numpy as jnp
import numpy as np
from jax import lax
from jax.experimental import pallas as pl
from jax.experimental.pallas import tpu as pltpu

F32 = jnp.float32
BF16 = jnp.bfloat16
HIGHEST = lax.Precision.HIGHEST

D_MODEL = 1024
DEPTH = 4
GRID_W = 64
GRID_SHIFT = 6
MLA_HEADS = 8
Q_RANK = 384
KV_RANK = 128
NOPE_DIM = 64
ROPE_DIM = 32
V_DIM = 64
QK_DIM = NOPE_DIM + ROPE_DIM
ROPE_THETA = 10000.0
NA_HEADS = 8
NA_HEAD_DIM = 64
NA_KH = 8
NA_KW = 16
NA_WIDTH = NA_HEADS * NA_HEAD_DIM
HY_ORDER = 2
HY_BANDS = 8
HY_EMB = 1 + 2 * HY_BANDS
HY_FF = 64
HY_TARGET = 1e-2
HY_FAST_DECAY = 0.3
HY_SLOW_DECAY = 1.5
HY_MOD_SHIFT = 0.05
N_GROUPS = 4
EXPERTS_PER_GROUP = 8
N_EXPERTS = N_GROUPS * EXPERTS_PER_GROUP
D_EXPERT = 256
EPS = 1e-6
NEG_INF = -1e30

LANES = 128
HEAD_PAD = LANES
MOD_ROWS = 8
VMEM_LIMIT = 56 * 1024 * 1024

MLA_SCALE = QK_DIM ** -0.5
NA_SCALE = NA_HEAD_DIM ** -0.5

NA_QROWS = 8
MOE_TILE = 256
TOK_TILE = 256


def _params(n_axes, vmem=VMEM_LIMIT):
    return pltpu.CompilerParams(dimension_semantics=("arbitrary",) * n_axes, vmem_limit_bytes=vmem)


def _mm(a, b):
    return jnp.dot(a.astype(BF16), b.astype(BF16), preferred_element_type=F32)


def _mm_nt(a, b):
    return lax.dot_general(a.astype(BF16), b.astype(BF16), (((1,), (1,)), ((), ())),
                           preferred_element_type=F32)


def _mm_hi(a, b):
    return jnp.dot(a, b, preferred_element_type=F32, precision=HIGHEST)


def _norm_mod(x, g, sc, sh):
    ms = jnp.mean(x * x, axis=-1, keepdims=True)
    return (x * lax.rsqrt(ms + EPS) * g) * (1.0 + sc) + sh


def _lane_iota(shape):
    return lax.broadcasted_iota(jnp.int32, shape, len(shape) - 1)


class _Layout:
    def __init__(self, batch, seq, dec_b, dec_l, past):
        self.batch, self.seq, self.dec_b, self.dec_l, self.past = batch, seq, dec_b, dec_l, past
        self.n_ctx = batch * seq
        self.n_dec = dec_b * dec_l
        self.n = self.n_ctx + self.n_dec
        self.tm = min(TOK_TILE, seq, dec_l)
        assert self.n_ctx % self.tm == 0 and dec_l % self.tm == 0
        assert self.n_ctx % dec_l == 0 and dec_l % GRID_W == 0
        assert 1 + dec_b <= MOD_ROWS

    def seg(self, i):
        a = self.n_ctx // self.tm
        b = self.dec_l // self.tm
        return jnp.where(i < a, 0, 1 + (i - a) // b)

    def mod_spec(self):
        return pl.BlockSpec((None, 1, D_MODEL), lambda i: (self.seg(i), 0, 0))


def _full(shape):
    nd = len(shape)
    return pl.BlockSpec(shape, lambda *_: (0,) * nd)


def _ada_kernel(ct_ref, w_ref, b_ref, o_ref, *, rows):
    c = ct_ref[...]
    s = c * jax.nn.sigmoid(c)
    w = w_ref[...]
    out = [jnp.sum(s[:, r:r + 1] * w, axis=0, keepdims=True) for r in range(rows)]
    out += [jnp.zeros_like(out[0])] * (MOD_ROWS - rows)
    o_ref[...] = jnp.concatenate(out, axis=0) + b_ref[...]


def _ada_call(cvec_t, ada_w, ada_b, rows):
    depth, d, n6 = ada_w.shape
    tn = 1536
    return pl.pallas_call(
        functools.partial(_ada_kernel, rows=rows),
        grid=(depth, n6 // tn),
        in_specs=[
            pl.BlockSpec((d, MOD_ROWS), lambda l, j: (0, 0)),
            pl.BlockSpec((None, d, tn), lambda l, j: (l, 0, j)),
            pl.BlockSpec((None, 1, tn), lambda l, j: (l, 0, j)),
        ],
        out_specs=pl.BlockSpec((None, MOD_ROWS, tn), lambda l, j: (l, 0, j)),
        out_shape=jax.ShapeDtypeStruct((depth, MOD_ROWS, n6), F32),
        compiler_params=_params(2),
        name="ada_mod",
    )(cvec_t, ada_w, ada_b.reshape(depth, 1, n6))


def _dense_kernel(*refs, has_norm, has_bias, has_resid):
    it = iter(refs)
    x_ref = next(it)
    if has_norm:
        g_ref, sc_ref, sh_ref = next(it), next(it), next(it)
    w_ref = next(it)
    b_ref = next(it) if has_bias else None
    if has_resid:
        r_ref, gate_ref = next(it), next(it)
    o_ref = next(it)
    x = x_ref[...]
    if has_norm:
        x = _norm_mod(x.astype(F32), g_ref[...], sc_ref[...], sh_ref[...])
    y = _mm(x, w_ref[...])
    if has_bias:
        y = y + b_ref[...]
    if has_resid:
        y = r_ref[...] + gate_ref[...] * y
    o_ref[...] = y.astype(o_ref.dtype)


def _dense_call(lay, x, w, *, norm=None, bias=None, resid=None, out_dtype=F32, name):
    n, k = x.shape
    m = w.shape[1]
    tm = lay.tm
    args, specs = [x], [pl.BlockSpec((tm, k), lambda i: (i, 0))]
    if norm is not None:
        g, sc, sh = norm
        args += [g, sc, sh]
        specs += [_full((1, k)), lay.mod_spec(), lay.mod_spec()]
    args.append(w)
    specs.append(_full((k, m)))
    if bias is not None:
        args.append(bias)
        specs.append(_full((1, m)))
    if resid is not None:
        r, gate = resid
        args += [r, gate]
        specs += [pl.BlockSpec((tm, m), lambda i: (i, 0)), lay.mod_spec()]
    return pl.pallas_call(
        functools.partial(_dense_kernel, has_norm=norm is not None, has_bias=bias is not None,
                          has_resid=resid is not None),
        grid=(n // tm,),
        in_specs=specs,
        out_specs=pl.BlockSpec((tm, m), lambda i: (i, 0)),
        out_shape=jax.ShapeDtypeStruct((n, m), out_dtype),
        compiler_params=_params(1),
        name=name,
    )(*args)


def _head_rms(xh, gain):
    ss = jnp.sum(xh * xh, axis=-1, keepdims=True) * (1.0 / QK_DIM)
    return xh * lax.rsqrt(ss + EPS) * gain


def _rope(xh, c, s_lo, s_hi):
    quarter = ROPE_DIM // 4
    return xh * c + pltpu.roll(xh, HEAD_PAD - quarter, 1) * s_lo + pltpu.roll(xh, quarter, 1) * s_hi


def _pair_rms(x, gain):
    lo = _lane_iota(x.shape) < NA_HEAD_DIM
    x2 = x * x
    s_lo = jnp.sum(jnp.where(lo, x2, 0.0), axis=-1, keepdims=True) * (1.0 / NA_HEAD_DIM)
    s_hi = jnp.sum(jnp.where(lo, 0.0, x2), axis=-1, keepdims=True) * (1.0 / NA_HEAD_DIM)
    r = jnp.where(lo, lax.rsqrt(s_lo + EPS), lax.rsqrt(s_hi + EPS))
    return x * r * gain


def _attn_in_kernel(x_ref, g_ref, sc_ref, sh_ref, wql_ref, wkvl_ref, wkr_ref, wna_ref, qn_ref, kvn_ref,
                    wqu_ref, wku_ref, wvu_ref, qg_ref, kg_ref, nqg_ref, nkg_ref, rc_ref, rlo_ref, rhi_ref,
                    q_ref, k_ref, v_ref, nq_ref, nk_ref, nv_ref, ckv_ref, krp_ref, *, ctx_tiles):
    latent = (pl.program_id(0) >= ctx_tiles).astype(F32)
    rc = 1.0 + latent * (rc_ref[...] - 1.0)
    rlo = latent * rlo_ref[...]
    rhi = latent * rhi_ref[...]

    h = _norm_mod(x_ref[...], g_ref[...], sc_ref[...], sh_ref[...]).astype(BF16)
    q_lat = _mm(h, wql_ref[...])
    kv_lat = _mm(h, wkvl_ref[...])
    krp = _mm(h, wkr_ref[...])
    na = _mm(h, wna_ref[...])

    def rms(x, g):
        return x * lax.rsqrt(jnp.mean(x * x, axis=-1, keepdims=True) + EPS) * g

    ckv = rms(kv_lat, kvn_ref[...])
    ckv_ref[...] = ckv
    krp_ref[...] = krp
    q_up = _mm(rms(q_lat, qn_ref[...]), wqu_ref[...])
    k_up = _mm(ckv, wku_ref[...])
    v_ref[...] = _mm(ckv, wvu_ref[...]).astype(v_ref.dtype)
    qg, kg = qg_ref[...], kg_ref[...]
    for hd in range(MLA_HEADS):
        sl = slice(hd * HEAD_PAD, (hd + 1) * HEAD_PAD)
        q_ref[:, sl] = _rope(_head_rms(q_up[:, sl], qg), rc, rlo, rhi).astype(q_ref.dtype)
        k_ref[:, sl] = _rope(_head_rms(k_up[:, sl] + krp, kg), rc, rlo, rhi).astype(k_ref.dtype)
    nqg, nkg = nqg_ref[...], nkg_ref[...]
    for j in range(NA_WIDTH // LANES):
        sl = slice(j * LANES, (j + 1) * LANES)
        nq_ref[:, sl] = _pair_rms(na[:, j * LANES:(j + 1) * LANES], nqg).astype(nq_ref.dtype)
        nk_ref[:, sl] = _pair_rms(na[:, NA_WIDTH + j * LANES:NA_WIDTH + (j + 1) * LANES], nkg)
    nv_ref[...] = na[:, 2 * NA_WIDTH:]


def _attn_in_call(lay, x, norm, w, rope):
    n, d = x.shape
    tm = lay.tm
    ctx_tiles = lay.n_ctx // tm
    dec_tiles = lay.dec_l // tm
    g, sc, sh = norm

    def rope_map(i):
        return (jnp.where(i < ctx_tiles, 0, (i - ctx_tiles) % dec_tiles), 0)

    tok = lambda width: pl.BlockSpec((tm, width), lambda i: (i, 0))
    hp = MLA_HEADS * HEAD_PAD
    in_specs = [tok(d), _full((1, d)), lay.mod_spec(), lay.mod_spec(),
                _full(w["wql"].shape), _full(w["wkvl"].shape), _full(w["wkr"].shape), _full(w["wna"].shape),
                _full((1, Q_RANK)), _full((1, KV_RANK)),
                _full(w["wqu"].shape), _full(w["wku"].shape), _full(w["wvu"].shape),
                _full((1, HEAD_PAD)), _full((1, HEAD_PAD)), _full((1, LANES)), _full((1, LANES)),
                pl.BlockSpec((tm, HEAD_PAD), rope_map), pl.BlockSpec((tm, HEAD_PAD), rope_map),
                pl.BlockSpec((tm, HEAD_PAD), rope_map)]
    out_shapes = [jax.ShapeDtypeStruct((n, hp), BF16), jax.ShapeDtypeStruct((n, hp), BF16),
                  jax.ShapeDtypeStruct((n, MLA_HEADS * V_DIM), BF16), jax.ShapeDtypeStruct((n, NA_WIDTH), BF16),
                  jax.ShapeDtypeStruct((n, NA_WIDTH), F32), jax.ShapeDtypeStruct((n, NA_WIDTH), F32),
                  jax.ShapeDtypeStruct((n, KV_RANK), F32), jax.ShapeDtypeStruct((n, HEAD_PAD), F32)]
    out_specs = [tok(s.shape[1]) for s in out_shapes]
    return pl.pallas_call(
        functools.partial(_attn_in_kernel, ctx_tiles=ctx_tiles),
        grid=(n // tm,),
        in_specs=in_specs,
        out_specs=out_specs,
        out_shape=out_shapes,
        compiler_params=_params(1),
        name="attn_in",
    )(x, g, sc, sh, w["wql"], w["wkvl"], w["wkr"], w["wna"], w["qn"], w["kvn"], w["wqu"], w["wku"], w["wvu"],
      w["qg"], w["kg"], w["nqg"], w["nkg"], *rope)


def _ctx_expand_kernel(ckv_ref, krp_ref, wku_ref, wvu_ref, kg_ref, k_ref, v_ref):
    ckv = ckv_ref[...]
    krp = krp_ref[...]
    k_up = _mm(ckv, wku_ref[...])
    v_ref[...] = _mm(ckv, wvu_ref[...]).astype(v_ref.dtype)
    kg = kg_ref[...]
    for hd in range(MLA_HEADS):
        sl = slice(hd * HEAD_PAD, (hd + 1) * HEAD_PAD)
        k_ref[:, sl] = _head_rms(k_up[:, sl] + krp, kg).astype(k_ref.dtype)


def _ctx_expand_call(ckv, krp, w):
    n = ckv.shape[0]
    tm = min(512, n)
    hp = MLA_HEADS * HEAD_PAD
    return pl.pallas_call(
        _ctx_expand_kernel,
        grid=(n // tm,),
        in_specs=[pl.BlockSpec((tm, KV_RANK), lambda i: (i, 0)), pl.BlockSpec((tm, HEAD_PAD), lambda i: (i, 0)),
                  _full(w["wku"].shape), _full(w["wvu"].shape), _full((1, HEAD_PAD))],
        out_specs=[pl.BlockSpec((tm, hp), lambda i: (i, 0)), pl.BlockSpec((tm, MLA_HEADS * V_DIM), lambda i: (i, 0))],
        out_shape=[jax.ShapeDtypeStruct((n, hp), BF16), jax.ShapeDtypeStruct((n, MLA_HEADS * V_DIM), BF16)],
        compiler_params=_params(1),
        name="ctx_expand",
    )(ckv, krp, w["wku"], w["wvu"], w["kg"])


def _softmax_pv(parts):
    m = functools.reduce(jnp.maximum, [jnp.max(s, axis=-1, keepdims=True) for s, _ in parts])
    acc, den = None, None
    for s, v in parts:
        e = jnp.exp(s - m)
        l = jnp.sum(e, axis=-1, keepdims=True)
        o = _mm(e, v)
        acc = o if acc is None else acc + o
        den = l if den is None else den + l
    return acc / den


def _ctx_attn_kernel(q_ref, k_ref, v_ref, nq_ref, nk_ref, nv_ref, o_ref):
    lo = _lane_iota((q_ref.shape[0], LANES)) < V_DIM
    for j in range(MLA_HEADS // 2):
        vp = v_ref[:, j * LANES:(j + 1) * LANES]
        outs = []
        for hd in (2 * j, 2 * j + 1):
            sl = slice(hd * HEAD_PAD, (hd + 1) * HEAD_PAD)
            s = _mm_nt(q_ref[:, sl], k_ref[:, sl]) * MLA_SCALE
            outs.append(_softmax_pv([(s, vp)]))
        o_ref[:, j * LANES:(j + 1) * LANES] = jnp.where(lo, outs[0], outs[1]).astype(o_ref.dtype)
    base = MLA_HEADS * V_DIM
    for j in range(NA_HEADS // 2):
        sl = slice(j * LANES, (j + 1) * LANES)
        qp = nq_ref[:, sl]
        kp = nk_ref[:, sl].astype(BF16)
        vp = nv_ref[:, sl].astype(BF16)
        outs = []
        for half in (0, 1):
            qm = jnp.where(lo if half == 0 else jnp.logical_not(lo), qp, jnp.zeros_like(qp))
            s = _mm_nt(qm, kp) * NA_SCALE
            outs.append(_softmax_pv([(s, vp)]))
        o_ref[:, base + j * LANES:base + (j + 1) * LANES] = jnp.where(lo, outs[0], outs[1]).astype(o_ref.dtype)


def _ctx_attn_call(lay, q, k, v, nq, nk, nv, out):
    s = lay.seq
    blk = lambda width: pl.BlockSpec((s, width), lambda b: (b, 0))
    return pl.pallas_call(
        _ctx_attn_body,
        grid=(lay.batch,),
        in_specs=[blk(q.shape[1]), blk(k.shape[1]), blk(v.shape[1]), blk(nq.shape[1]), blk(nk.shape[1]),
                  blk(nv.shape[1]), pl.BlockSpec(memory_space=pl.ANY)],
        out_specs=blk(out.shape[1]),
        out_shape=jax.ShapeDtypeStruct(out.shape, out.dtype),
        input_output_aliases={6: 0},
        compiler_params=_params(1),
        name="ctx_attn",
    )(q, k, v, nq, nk, nv, out)


def _ctx_attn_body(q_ref, k_ref, v_ref, nq_ref, nk_ref, nv_ref, alias_ref, o_ref):
    del alias_ref
    _ctx_attn_kernel(q_ref, k_ref, v_ref, nq_ref, nk_ref, nv_ref, o_ref)


def _mla_attn_kernel(q_ref, k_ref, v_ref, kc_ref, vc_ref, alias_ref, o_ref):
    del alias_ref
    lo = _lane_iota((q_ref.shape[0], LANES)) < V_DIM
    for j in range(MLA_HEADS // 2):
        vp = v_ref[:, j * LANES:(j + 1) * LANES]
        vcp = vc_ref[:, j * LANES:(j + 1) * LANES]
        outs = []
        for hd in (2 * j, 2 * j + 1):
            sl = slice(hd * HEAD_PAD, (hd + 1) * HEAD_PAD)
            qh = q_ref[:, sl]
            s1 = _mm_nt(qh, k_ref[:, sl]) * MLA_SCALE
            s2 = _mm_nt(qh, kc_ref[:, sl]) * MLA_SCALE
            outs.append(_softmax_pv([(s1, vp), (s2, vcp)]))
        o_ref[:, j * LANES:(j + 1) * LANES] = jnp.where(lo, outs[0], outs[1]).astype(o_ref.dtype)


def _mla_attn_call(lay, q, k, v, kc, vc, out):
    L, P = lay.dec_l, lay.past
    tq = min(256, L)
    qb = L // tq
    off_q = lay.n_ctx // tq
    off_l = lay.n_ctx // L
    hp = MLA_HEADS * HEAD_PAD
    vw = MLA_HEADS * V_DIM
    return pl.pallas_call(
        _mla_attn_kernel,
        grid=(lay.dec_b, qb),
        in_specs=[pl.BlockSpec((tq, hp), lambda b, i: (off_q + b * qb + i, 0)),
                  pl.BlockSpec((L, hp), lambda b, i: (off_l + b, 0)),
                  pl.BlockSpec((L, vw), lambda b, i: (off_l + b, 0)),
                  pl.BlockSpec((P, hp), lambda b, i: (b, 0)),
                  pl.BlockSpec((P, vw), lambda b, i: (b, 0)),
                  pl.BlockSpec(memory_space=pl.ANY)],
        out_specs=pl.BlockSpec((tq, vw), lambda b, i: (off_q + b * qb + i, 0)),
        out_shape=jax.ShapeDtypeStruct(out.shape, out.dtype),
        input_output_aliases={5: 0},
        compiler_params=_params(2),
        name="mla_attn",
    )(q, k, v, kc, vc, out)


def _na_bias_kernel(rpb_ref, o_ref):
    h = pl.program_id(0)
    n_dy, n_dx = 2 * NA_KH - 1, 2 * NA_KW - 1
    qc = lax.broadcasted_iota(jnp.int32, (GRID_W, LANES), 0)
    kc = _lane_iota((GRID_W, LANES)) & (GRID_W - 1)
    dx = jnp.clip(kc - qc, -(NA_KW - 1), NA_KW - 1) + NA_KW - 1
    cs = jnp.clip(qc - NA_KW // 2, 0, GRID_W - NA_KW)
    col_ok = (kc >= cs) & (kc < cs + NA_KW)
    tiles = []
    for dy in range(n_dy):
        acc = jnp.zeros((GRID_W, LANES), F32)
        for i in range(n_dx):
            acc = jnp.where(dx == i, rpb_ref[(h * n_dy + dy) * n_dx + i], acc)
        tiles.append(jnp.where(col_ok, acc, NEG_INF))
    zero = jnp.zeros((GRID_W, LANES), F32)
    tiles = [zero] + tiles + [zero]
    lo = _lane_iota((GRID_W, LANES)) < GRID_W
    for i in range(n_dy + 1):
        o_ref[i] = jnp.where(lo, tiles[i], tiles[i + 1])


def _na_bias_call(rpb):
    n_pairs = 2 * NA_KH
    return pl.pallas_call(
        _na_bias_kernel,
        grid=(NA_HEADS,),
        in_specs=[pl.BlockSpec(memory_space=pltpu.SMEM)],
        out_specs=pl.BlockSpec((n_pairs, GRID_W, LANES), lambda h: (h, 0, 0)),
        out_shape=jax.ShapeDtypeStruct((NA_HEADS * n_pairs, GRID_W, LANES), F32),
        compiler_params=_params(1),
        name="na_bias",
    )(rpb.reshape(-1))


def _na_attn_kernel(q_ref, k_ref, v_ref, kc_ref, vc_ref, tp_ref, alias_ref, o_ref, *, rows, win):
    del alias_ref
    g = pl.program_id(1)
    n_pairs = 2 * NA_KH
    nq = NA_QROWS * GRID_W
    nk = win * GRID_W
    ws = jnp.clip(NA_QROWS * g - NA_KH // 2, 0, rows - win)
    start = pl.multiple_of(ws * GRID_W, GRID_W)
    kwin = k_ref[pl.ds(start, nk), :].astype(BF16)
    vwin = v_ref[pl.ds(start, nk), :].astype(BF16)
    kctx = kc_ref[...].astype(BF16)
    vctx = vc_ref[...].astype(BF16)

    qr = NA_QROWS * g + (lax.broadcasted_iota(jnp.int32, (nq, nk), 0) >> GRID_SHIFT)
    kr = ws + (_lane_iota((nq, nk)) >> GRID_SHIFT)
    rs = jnp.clip(qr - NA_KH // 2, 0, rows - NA_KH)
    row_ok = (kr >= rs) & (kr < rs + NA_KH)

    lo = _lane_iota((nq, LANES)) < NA_HEAD_DIM
    for j in range(NA_HEADS // 2):
        sl = slice(j * LANES, (j + 1) * LANES)
        qp = q_ref[:, sl]
        kp, vp, kcp, vcp = kwin[:, sl], vwin[:, sl], kctx[:, sl], vctx[:, sl]
        outs = []
        for half in (0, 1):
            hd = 2 * j + half
            bias_rows = []
            for a in range(NA_QROWS):
                blocks = []
                for p in range(win // 2):
                    dy = ws + 2 * p - (NA_QROWS * g + a) + NA_KH - 1
                    idx = jnp.clip(dy + 1, 0, n_pairs - 1)
                    blocks.append(tp_ref[hd * n_pairs + idx])
                bias_rows.append(jnp.concatenate(blocks, axis=1))
            bias = jnp.concatenate(bias_rows, axis=0)
            qm = jnp.where(lo if half == 0 else jnp.logical_not(lo), qp, jnp.zeros_like(qp))
            s_w = jnp.where(row_ok, _mm_nt(qm, kp) * NA_SCALE + bias, NEG_INF)
            s_c = _mm_nt(qm, kcp) * NA_SCALE
            outs.append(_softmax_pv([(s_w, vp), (s_c, vcp)]))
        o_ref[:, sl] = jnp.where(lo, outs[0], outs[1]).astype(o_ref.dtype)


def _na_attn_call(lay, nq, nk, nv, kc, vc, tp, out):
    L, P = lay.dec_l, lay.past
    rows = L // GRID_W
    assert rows % NA_QROWS == 0
    win = min(2 * NA_KH, rows)
    groups = rows // NA_QROWS
    tq = NA_QROWS * GRID_W
    off_q = lay.n_ctx // tq
    off_l = lay.n_ctx // L
    n_out_blk = out.shape[1] // NA_WIDTH
    return pl.pallas_call(
        functools.partial(_na_attn_kernel, rows=rows, win=win),
        grid=(lay.dec_b, groups),
        in_specs=[pl.BlockSpec((tq, NA_WIDTH), lambda b, g: (off_q + b * groups + g, 0)),
                  pl.BlockSpec((L, NA_WIDTH), lambda b, g: (off_l + b, 0)),
                  pl.BlockSpec((L, NA_WIDTH), lambda b, g: (off_l + b, 0)),
                  pl.BlockSpec((P, NA_WIDTH), lambda b, g: (b, 0)),
                  pl.BlockSpec((P, NA_WIDTH), lambda b, g: (b, 0)),
                  _full(tp.shape),
                  pl.BlockSpec(memory_space=pl.ANY)],
        out_specs=pl.BlockSpec((tq, NA_WIDTH), lambda b, g: (off_q + b * groups + g, n_out_blk - 1)),
        out_shape=jax.ShapeDtypeStruct(out.shape, out.dtype),
        input_output_aliases={6: 0},
        compiler_params=_params(2),
        name="na_attn",
    )(nq, nk, nv, kc, vc, tp, out)


def _hy_filter_kernel(z_ref, w1_ref, b1_ref, fr_ref, w2_ref, b2_ref, w3a_ref, w3b_ref, w3c_ref, w3d_ref,
                      dl_ref, t_ref, alt_ref, wf_ref, c_ref, s_ref, kr_ref, ki_ref, kl_ref):
    h = jnp.sin(fr_ref[0:1, :] * (_mm_hi(z_ref[...], w1_ref[...]) + b1_ref[...]))
    h = jnp.sin(fr_ref[1:2, :] * (_mm_hi(h, w2_ref[...]) + b2_ref[...]))
    window = jnp.exp(-t_ref[...] * dl_ref[...]) + HY_MOD_SHIFT
    first = lax.broadcasted_iota(jnp.int32, window.shape, 0) == 0
    alt = alt_ref[...]
    wf = wf_ref[...]
    fwd = (w3a_ref, w3b_ref)
    bwd = (w3c_ref, w3d_ref)
    for o in range(HY_ORDER):
        hf = _mm_hi(h, fwd[o][...]) * window
        hb = jnp.where(first, 0.0, _mm_hi(h, bwd[o][...]) * window)
        norm = jnp.sum(jnp.abs(hf), axis=0, keepdims=True) + jnp.sum(jnp.abs(hb), axis=0, keepdims=True) + EPS
        a = (hf + hb) / norm
        d = (hf - hb) / norm
        kr_ref[o] = _mm(c_ref[...], a) * wf
        ki_ref[o] = -_mm(s_ref[...], d) * wf
        kl_ref[o] = jnp.sum(a * alt, axis=0, keepdims=True) * (0.5 / a.shape[0])


def _hy_filter_call(L, z, p, consts):
    d = D_MODEL
    dc = 256
    nc = d // dc
    w3 = p["w3"]
    w3_spec = lambda k: pl.BlockSpec((HY_FF, dc), lambda c: (0, k * nc + c))
    zpad = z.shape[1]
    return pl.pallas_call(
        _hy_filter_kernel,
        grid=(nc,),
        in_specs=[_full((L, zpad)), _full((zpad, HY_FF)), _full((1, HY_FF)), _full((2, HY_FF)),
                  _full((HY_FF, HY_FF)), _full((1, HY_FF)), w3_spec(0), w3_spec(1), w3_spec(2), w3_spec(3),
                  pl.BlockSpec((1, dc), lambda c: (0, c)), _full((L, 1)), _full((L, 1)), _full((L, 1)),
                  _full((L, L)), _full((L, L))],
        out_specs=[pl.BlockSpec((HY_ORDER, L, dc), lambda c: (0, 0, c)),
                   pl.BlockSpec((HY_ORDER, L, dc), lambda c: (0, 0, c)),
                   pl.BlockSpec((HY_ORDER, 1, dc), lambda c: (0, 0, c))],
        out_shape=[jax.ShapeDtypeStruct((HY_ORDER, L, d), F32), jax.ShapeDtypeStruct((HY_ORDER, L, d), F32),
                   jax.ShapeDtypeStruct((HY_ORDER, 1, d), F32)],
        compiler_params=_params(1),
        name="hy_filter",
    )(z, p["w1"], p["b1"], p["freq"], p["w2"], p["b2"], w3, w3, w3, w3,
      consts["deltas"], consts["t"], consts["alt"], consts["wf"], consts["C"], consts["S"])


def _hy_stage_kernel(u_ref, g_ref, cwu_ref, cbu_ref, cwg_ref, cbg_ref, kr_ref, ki_ref, kl_ref, skip_ref, alt_ref,
                     c_ref, s_ref, alias_ref, o_ref, *, conv_u):
    del alias_ref
    L = u_ref.shape[0]
    row = lax.broadcasted_iota(jnp.int32, u_ref.shape, 0)
    first, last = row == 0, row == L - 1

    def short_conv(ref, w_ref, b_ref):
        x = ref[...]
        prev = jnp.where(first, 0.0, pltpu.roll(x, 1, 0))
        nxt = jnp.where(last, 0.0, pltpu.roll(x, L - 1, 0))
        w = w_ref[...]
        return prev * w[0:1] + x * w[1:2] + nxt * w[2:3] + b_ref[...]

    u = short_conv(u_ref, cwu_ref, cbu_ref) if conv_u else u_ref[...]
    alt = alt_ref[...]
    ub = u.astype(BF16)
    ur = jnp.dot(c_ref[...], ub, preferred_element_type=F32)
    us = jnp.dot(s_ref[...], ub, preferred_element_type=F32)
    ul = jnp.sum(u * alt, axis=0, keepdims=True)
    kr, ki = kr_ref[...], ki_ref[...]
    yr = (ur * kr + us * ki).astype(BF16)
    zi = (us * kr - ur * ki).astype(BF16)
    y = jnp.dot(c_ref[...], yr, preferred_element_type=F32) + jnp.dot(s_ref[...], zi, preferred_element_type=F32)
    y = y + alt * (ul * kl_ref[...]) + u * skip_ref[...]
    o_ref[...] = (short_conv(g_ref, cwg_ref, cbg_ref) * y).astype(o_ref.dtype)


def _hy_stage_call(u, u_blk, gate, g_blk, cwu, cbu, cwg, cbg, kr, ki, kl, skip, consts, *, conv_u, L, nseq,
                   row_off, out, td):
    d = D_MODEL
    nc = d // td
    off = row_off // L
    one = pl.Buffered(1)
    chan = lambda rows: pl.BlockSpec((rows, td), lambda c, b: (0, c))
    return pl.pallas_call(
        functools.partial(_hy_stage_kernel, conv_u=conv_u),
        grid=(nc, nseq),
        in_specs=[pl.BlockSpec((L, td), lambda c, b: (off + b, u_blk * nc + c)),
                  pl.BlockSpec((L, td), lambda c, b: (off + b, g_blk * nc + c)),
                  chan(3), chan(1), chan(3), chan(1),
                  pl.BlockSpec((L, td), lambda c, b: (0, c), pipeline_mode=one),
                  pl.BlockSpec((L, td), lambda c, b: (0, c), pipeline_mode=one),
                  chan(1), chan(1),
                  _full((L, 1)),
                  pl.BlockSpec((L, L), lambda c, b: (0, 0), pipeline_mode=one),
                  pl.BlockSpec((L, L), lambda c, b: (0, 0), pipeline_mode=one),
                  pl.BlockSpec(memory_space=pl.ANY)],
        out_specs=pl.BlockSpec((L, td), lambda c, b: (off + b, c)),
        out_shape=jax.ShapeDtypeStruct(out.shape, out.dtype),
        input_output_aliases={13: 0},
        compiler_params=_params(2),
        name="hy_stage_%d" % L,
    )(u, gate, cwu, cbu, cwg, cbg, kr, ki, kl, skip, consts["alt"], consts["C"], consts["S"], out)


def _dft_consts(L):
    f = jnp.arange(L, dtype=jnp.int32)
    ang = ((f[:, None] * f[None, :]) % (2 * L)).astype(F32) * (math.pi / L)
    t = jnp.linspace(0.0, 1.0, L, dtype=F32)[:, None]
    w = 2.0 * math.pi * jnp.arange(L, dtype=F32)[:, None] / L
    bands = jnp.linspace(1e-4, HY_BANDS - 1, HY_BANDS, dtype=F32)[None, :]
    z = jnp.concatenate([t, jnp.cos(bands * w), -jnp.sin(bands * w)], axis=-1)
    z = jnp.pad(z, ((0, 0), (0, 32 - HY_EMB)))
    deltas = jnp.abs(jnp.linspace(math.log(HY_TARGET) / HY_FAST_DECAY, math.log(HY_TARGET) / HY_SLOW_DECAY,
                                  D_MODEL, dtype=F32))[None, :]
    alt = (1.0 - 2.0 * (f % 2).astype(F32))[:, None]
    wf = jnp.where(f == 0, 0.5 / L, 1.0 / L).astype(F32)[:, None]
    return {"C": jnp.cos(ang).astype(BF16), "S": jnp.sin(ang).astype(BF16), "t": t, "z": z, "deltas": deltas,
            "alt": alt, "wf": wf}


def _route_kernel(x_ref, g_ref, sc_ref, sh_ref, wr_ref, br_ref, h_ref, meta_ref, cnt_ref, base_ref):
    i = pl.program_id(0)
    tm = x_ref.shape[0]

    @pl.when(i == 0)
    def _():
        base_ref[...] = jnp.zeros_like(base_ref)

    h = _norm_mod(x_ref[...], g_ref[...], sc_ref[...], sh_ref[...])
    h_ref[...] = h
    logits = _mm_hi(h, wr_ref[...]) + br_ref[...]
    lane_i = _lane_iota(logits.shape)
    lane = lane_i.astype(F32)
    grp_of_lane = (lane_i >> 3).astype(F32)
    big = 1e6
    is_g = (lane_i >= N_EXPERTS) & (lane_i < N_EXPERTS + N_GROUPS)
    glog = jnp.where(is_g, logits, NEG_INF)
    gmax = jnp.max(glog, axis=-1, keepdims=True)
    g_w = 1.0 / jnp.sum(jnp.exp(glog - gmax), axis=-1, keepdims=True)
    g_i = jnp.min(jnp.where(glog == gmax, lane - N_EXPERTS, big), axis=-1, keepdims=True)
    in_grp = (lane_i < N_EXPERTS) & (grp_of_lane == g_i)
    elog = jnp.where(in_grp, logits, NEG_INF)
    m1 = jnp.max(elog, axis=-1, keepdims=True)
    i1 = jnp.min(jnp.where(elog == m1, lane, big), axis=-1, keepdims=True)
    elog2 = jnp.where(lane == i1, NEG_INF, elog)
    m2 = jnp.max(elog2, axis=-1, keepdims=True)
    i2 = jnp.min(jnp.where(elog2 == m2, lane, big), axis=-1, keepdims=True)
    a2 = jnp.exp(m2 - m1)
    w1 = g_w / (1.0 + a2)
    w2 = g_w * a2 / (1.0 + a2)

    oh1 = jnp.where(lane == i1, 1.0, 0.0).astype(BF16)
    oh2 = jnp.where(lane == i2, 1.0, 0.0).astype(BF16)
    tri = jnp.where(lax.broadcasted_iota(jnp.int32, (tm, tm), 0) > lax.broadcasted_iota(jnp.int32, (tm, tm), 1),
                    1.0, 0.0).astype(BF16)
    cum1 = jnp.dot(tri, oh1, preferred_element_type=F32)
    cum2 = jnp.dot(tri, oh2, preferred_element_type=F32)
    tot1 = jnp.sum(oh1.astype(F32), axis=0, keepdims=True)
    tot2 = jnp.sum(oh2.astype(F32), axis=0, keepdims=True)
    base = base_ref[...]
    r1 = jnp.sum(jnp.where(lane == i1, base + cum1, 0.0), axis=-1, keepdims=True)
    r2 = jnp.sum(jnp.where(lane == i2, base + tot1 + cum2, 0.0), axis=-1, keepdims=True)
    base = base + tot1 + tot2
    base_ref[...] = base
    cnt_ref[...] = base
    cols = [i1, i2, r1, r2, w1, w2]
    meta = jnp.zeros(logits.shape, F32)
    for c, val in enumerate(cols):
        meta = jnp.where(lane_i == c, val, meta)
    meta_ref[...] = meta


def _route_call(lay, x, norm, wr, br):
    n, d = x.shape
    tm = lay.tm
    g, sc, sh = norm
    tok = lambda width: pl.BlockSpec((tm, width), lambda i: (i, 0))
    return pl.pallas_call(
        _route_kernel,
        grid=(n // tm,),
        in_specs=[tok(d), _full((1, d)), lay.mod_spec(), lay.mod_spec(), _full((d, LANES)), _full((1, LANES))],
        out_specs=[tok(d), tok(LANES), _full((1, LANES))],
        out_shape=[jax.ShapeDtypeStruct((n, d), F32), jax.ShapeDtypeStruct((n, LANES), F32),
                   jax.ShapeDtypeStruct((1, LANES), F32)],
        scratch_shapes=[pltpu.VMEM((1, LANES), F32)],
        compiler_params=_params(1),
        name="moe_route",
    )(x, g, sc, sh, wr, br)


def _row_copy(src_ref, s, dst_ref, d, sem):
    return pltpu.make_async_copy(src_ref.at[pl.ds(s, 1)], dst_ref.at[pl.ds(d, 1)], sem)


def _dispatch_kernel(dest_ref, h_ref, xs_in_ref, xs_ref, sem):
    del xs_in_ref
    tm = h_ref.shape[0]

    def issue(t, c):
        _row_copy(h_ref, t, xs_ref, dest_ref[0, t], sem).start()
        _row_copy(h_ref, t, xs_ref, dest_ref[0, tm + t], sem).start()
        return c

    lax.fori_loop(0, tm, issue, 0)

    def drain(t, c):
        _row_copy(h_ref, 0, xs_ref, 0, sem).wait()
        return c

    lax.fori_loop(0, 2 * tm, drain, 0)


def _dispatch_call(lay, h, dest, xs0):
    n, d = h.shape
    tm = lay.tm
    return pl.pallas_call(
        _dispatch_kernel,
        grid=(n // tm,),
        in_specs=[pl.BlockSpec((None, 1, 2 * tm), lambda i: (i, 0, 0), memory_space=pltpu.SMEM),
                  pl.BlockSpec((tm, d), lambda i: (i, 0)),
                  pl.BlockSpec(memory_space=pl.ANY)],
        out_specs=pl.BlockSpec(memory_space=pl.ANY),
        out_shape=jax.ShapeDtypeStruct(xs0.shape, xs0.dtype),
        scratch_shapes=[pltpu.SemaphoreType.DMA(())],
        input_output_aliases={2: 0},
        compiler_params=_params(1),
        name="moe_dispatch",
    )(dest, h, xs0)


def _expert_kernel(te_ref, xs_ref, wg_ref, wu_ref, wd_ref, ys_ref):
    j = pl.program_id(0)
    used = te_ref[pl.num_programs(0)]

    @pl.when(j < used)
    def _():
        x = xs_ref[...].astype(BF16)
        hg = _mm(x, wg_ref[...])
        hu = _mm(x, wu_ref[...])
        act = hg * jax.nn.sigmoid(hg) * hu
        ys_ref[...] = _mm(act, wd_ref[...])

    @pl.when(j >= used)
    def _():
        ys_ref[...] = jnp.zeros_like(ys_ref)


def _expert_call(tile_expert, xs, wg, wu, wd):
    r, d = xs.shape
    f = wg.shape[-1]
    tiles = r // MOE_TILE
    grid_spec = pltpu.PrefetchScalarGridSpec(
        num_scalar_prefetch=1,
        grid=(tiles,),
        in_specs=[pl.BlockSpec((MOE_TILE, d), lambda j, te: (j, 0)),
                  pl.BlockSpec((None, d, f), lambda j, te: (te[j], 0, 0)),
                  pl.BlockSpec((None, d, f), lambda j, te: (te[j], 0, 0)),
                  pl.BlockSpec((None, f, d), lambda j, te: (te[j], 0, 0))],
        out_specs=pl.BlockSpec((MOE_TILE, d), lambda j, te: (j, 0)),
    )
    return pl.pallas_call(
        _expert_kernel,
        grid_spec=grid_spec,
        out_shape=jax.ShapeDtypeStruct((r, d), F32),
        compiler_params=_params(1),
        name="moe_experts",
    )(tile_expert, xs, wg, wu, wd)


def _combine_kernel(dest_ref, x_ref, gate_ref, meta_ref, ys_ref, o_ref, buf_ref, sem):
    tm = x_ref.shape[0]

    def issue(t, c):
        _row_copy(ys_ref, dest_ref[0, t], buf_ref.at[0], t, sem).start()
        _row_copy(ys_ref, dest_ref[0, tm + t], buf_ref.at[1], t, sem).start()
        return c

    lax.fori_loop(0, tm, issue, 0)

    def drain(t, c):
        _row_copy(ys_ref, 0, buf_ref.at[0], 0, sem).wait()
        return c

    lax.fori_loop(0, 2 * tm, drain, 0)
    meta = meta_ref[...]
    w1 = meta[:, 4:5]
    w2 = meta[:, 5:6]
    o_ref[...] = x_ref[...] + gate_ref[...] * (w1 * buf_ref[0] + w2 * buf_ref[1])


def _combine_call(lay, x, gate, meta, dest, ys):
    n, d = x.shape
    tm = lay.tm
    tok = lambda width: pl.BlockSpec((tm, width), lambda i: (i, 0))
    return pl.pallas_call(
        _combine_kernel,
        grid=(n // tm,),
        in_specs=[pl.BlockSpec((None, 1, 2 * tm), lambda i: (i, 0, 0), memory_space=pltpu.SMEM),
                  tok(d), lay.mod_spec(), tok(LANES), pl.BlockSpec(memory_space=pl.ANY)],
        out_specs=tok(d),
        out_shape=jax.ShapeDtypeStruct((n, d), F32),
        scratch_shapes=[pltpu.VMEM((2, tm, d), F32), pltpu.SemaphoreType.DMA(())],
        compiler_params=_params(1),
        name="moe_combine",
    )(dest, x, gate, meta, ys)


def _moe_layer(lay, x, norm, gate, wr, br, wg, wu, wd):
    n = x.shape[0]
    tm = lay.tm
    h, meta, counts = _route_call(lay, x, norm, wr, br)
    cnt = counts[0, :N_EXPERTS].astype(jnp.int32)
    padded = ((cnt + MOE_TILE - 1) // MOE_TILE) * MOE_TILE
    ends = jnp.cumsum(padded)
    starts = ends - padded
    e = meta[:, 0:2].astype(jnp.int32)
    dest = starts[e] + meta[:, 2:4].astype(jnp.int32)
    dest = dest.reshape(n // tm, tm, 2).transpose(0, 2, 1).reshape(n // tm, 1, 2 * tm)
    rows = 2 * n + N_EXPERTS * MOE_TILE
    tiles = rows // MOE_TILE
    tile_start = jnp.arange(tiles, dtype=jnp.int32) * MOE_TILE
    tile_expert = jnp.sum((tile_start[:, None] >= ends[None, :]).astype(jnp.int32), axis=1)
    used = ends[-1] // MOE_TILE
    last = jnp.take(tile_expert, jnp.maximum(used - 1, 0))
    tile_expert = jnp.where(tile_start // MOE_TILE < used, tile_expert, last)
    tile_expert = jnp.concatenate([tile_expert, used[None]]).astype(jnp.int32)
    xs = _dispatch_call(lay, h, dest, jnp.zeros((rows, x.shape[1]), F32))
    ys = _expert_call(tile_expert, xs, wg, wu, wd)
    return _combine_call(lay, x, gate, meta, dest, ys)


def _attn_weights(a, attn_w_in, mla_q_norm, mla_kv_norm, mla_w_q_up, mla_w_kv_up, mla_q_gain, mla_k_gain,
                  na_q_gain, na_k_gain):
    w_in = attn_w_in[a]
    o1, o2, o3 = Q_RANK, Q_RANK + KV_RANK, Q_RANK + KV_RANK + ROPE_DIM
    pad_head = lambda g: jnp.pad(g, (0, HEAD_PAD - QK_DIM))[None, :]
    wkr = jnp.zeros((D_MODEL, HEAD_PAD), F32).at[:, NOPE_DIM:QK_DIM].set(w_in[:, o2:o3])
    wqu = jnp.pad(mla_w_q_up[a].reshape(Q_RANK, MLA_HEADS, QK_DIM), ((0, 0), (0, 0), (0, HEAD_PAD - QK_DIM)))
    wkv = mla_w_kv_up[a].reshape(KV_RANK, MLA_HEADS, NOPE_DIM + V_DIM)
    wku = jnp.pad(wkv[:, :, :NOPE_DIM], ((0, 0), (0, 0), (0, HEAD_PAD - NOPE_DIM)))
    return {
        "wql": w_in[:, :o1].astype(BF16), "wkvl": w_in[:, o1:o2].astype(BF16), "wkr": wkr.astype(BF16),
        "wna": w_in[:, o3:].astype(BF16),
        "qn": mla_q_norm[a][None, :], "kvn": mla_kv_norm[a][None, :],
        "wqu": wqu.reshape(Q_RANK, MLA_HEADS * HEAD_PAD).astype(BF16),
        "wku": wku.reshape(KV_RANK, MLA_HEADS * HEAD_PAD).astype(BF16),
        "wvu": wkv[:, :, NOPE_DIM:].reshape(KV_RANK, MLA_HEADS * V_DIM).astype(BF16),
        "qg": pad_head(mla_q_gain[a]), "kg": pad_head(mla_k_gain[a]),
        "nqg": jnp.tile(na_q_gain[a], LANES // NA_HEAD_DIM)[None, :],
        "nkg": jnp.tile(na_k_gain[a], LANES // NA_HEAD_DIM)[None, :],
    }


def _rope_tables(L):
    t = jnp.arange(L)
    quarter = ROPE_DIM // 4
    inv = ROPE_THETA ** (-jnp.arange(quarter, dtype=F32) / quarter)
    ang_r = (t // GRID_W).astype(F32)[:, None] * inv[None, :]
    ang_c = (t % GRID_W).astype(F32)[:, None] * inv[None, :]
    zeros = jnp.zeros((L, quarter), F32)
    ones_nope = jnp.ones((L, NOPE_DIM), F32)
    pad = jnp.zeros((L, HEAD_PAD - QK_DIM), F32)
    cos = jnp.concatenate([ones_nope, jnp.cos(ang_r), jnp.cos(ang_r), jnp.cos(ang_c), jnp.cos(ang_c), pad], axis=1)
    s_lo = jnp.concatenate([0 * ones_nope, -jnp.sin(ang_r), zeros, -jnp.sin(ang_c), zeros, pad], axis=1)
    s_hi = jnp.concatenate([0 * ones_nope, zeros, jnp.sin(ang_r), zeros, jnp.sin(ang_c), pad], axis=1)
    return cos, s_lo, s_hi


def kernel(x_prompt, x_sample, cache_mla_ckv, cache_mla_krope, cache_na_k, cache_na_v, c, c_ctx, ada_w, ada_b, norm_mix, norm_ffn, attn_w_in, mla_q_norm, mla_kv_norm, mla_w_q_up, mla_w_kv_up, mla_q_gain, mla_k_gain, na_q_gain, na_k_gain, na_rpb, attn_w_out, hy_w_in, hy_b_in, hy_conv_w, hy_conv_b, hy_filt_w1, hy_filt_b1, hy_filt_freq, hy_filt_w2, hy_filt_b2, hy_filt_w3, hy_skip, hy_w_out, moe_w_group, moe_b_group, moe_w_router, moe_b_router, moe_w_gate, moe_w_up, moe_w_down):
    batch, seq, d = x_prompt.shape
    dec_b, dec_l, _ = x_sample.shape
    past = cache_mla_ckv.shape[2]
    depth = ada_w.shape[0]
    lay = _Layout(batch, seq, dec_b, dec_l, past)
    n_ctx = lay.n_ctx

    x = jnp.concatenate([x_prompt.reshape(n_ctx, d), x_sample.reshape(lay.n_dec, d)], axis=0)
    cvec = jnp.concatenate([c_ctx[None, :], c], axis=0)
    cvec_t = jnp.pad(cvec, ((0, MOD_ROWS - cvec.shape[0]), (0, 0))).T
    mods = _ada_call(cvec_t, ada_w, ada_b, 1 + dec_b)

    def mod(l, j):
        return mods[l, :, j * d:(j + 1) * d].reshape(MOD_ROWS, 1, d)

    rope = _rope_tables(dec_l)
    consts = {L: _dft_consts(L) for L in sorted({seq, dec_l})}
    ckv_out, krope_out, nk_out, nv_out = [], [], [], []

    for l in range(depth):
        sh1, sc1, g1, sh2, sc2, g2 = [mod(l, j) for j in range(6)]
        norm1 = (norm_mix[l][None, :], sc1, sh1)
        if l % 2 == 0:
            a = l // 2
            w = _attn_weights(a, attn_w_in, mla_q_norm, mla_kv_norm, mla_w_q_up, mla_w_kv_up, mla_q_gain,
                              mla_k_gain, na_q_gain, na_k_gain)
            q, k, v, nq, nk, nv, ckv, krp = _attn_in_call(lay, x, norm1, w, rope)
            ckv_out.append(ckv[:n_ctx].reshape(batch, seq, KV_RANK))
            krope_out.append(krp[:n_ctx, NOPE_DIM:QK_DIM].reshape(batch, seq, ROPE_DIM))
            nk_out.append(nk[:n_ctx].reshape(batch, seq, NA_HEADS, NA_HEAD_DIM))
            nv_out.append(nv[:n_ctx].reshape(batch, seq, NA_HEADS, NA_HEAD_DIM))
            krp_c = jnp.pad(cache_mla_krope[:, a].reshape(dec_b * past, ROPE_DIM),
                            ((0, 0), (NOPE_DIM, HEAD_PAD - QK_DIM)))
            kc, vc = _ctx_expand_call(cache_mla_ckv[:, a].reshape(dec_b * past, KV_RANK), krp_c, w)
            tp = _na_bias_call(na_rpb[a])
            att = jnp.zeros((lay.n, MLA_HEADS * V_DIM + NA_WIDTH), BF16)
            att = _ctx_attn_call(lay, q, k, v, nq, nk, nv, att)
            att = _mla_attn_call(lay, q, k, v, kc, vc, att)
            att = _na_attn_call(lay, nq, nk, nv, cache_na_k[:, a].reshape(dec_b * past, NA_WIDTH),
                                cache_na_v[:, a].reshape(dec_b * past, NA_WIDTH), tp, att)
            x = _dense_call(lay, att, attn_w_out[a].astype(BF16), resid=(x, g1), name="attn_out")
        else:
            j = l // 2
            p3 = _dense_call(lay, x, hy_w_in[j].astype(BF16), norm=norm1, bias=hy_b_in[j][None, :], name="hy_in")
            cw = hy_conv_w[j].reshape(3, 3, d).transpose(1, 0, 2)
            cb = hy_conv_b[j].reshape(3, 1, d)
            skip = hy_skip[j].reshape(HY_ORDER, 1, d)
            filt = {"w1": jnp.pad(hy_filt_w1[j], ((0, 32 - HY_EMB), (0, 0))), "b1": hy_filt_b1[j][None, :],
                    "freq": hy_filt_freq[j], "w2": hy_filt_w2[j], "b2": hy_filt_b2[j][None, :],
                    "w3": hy_filt_w3[j]}
            z1 = jnp.zeros((lay.n, d), F32)
            z = jnp.zeros((lay.n, d), BF16)
            for L, nseq, off, td in ((seq, batch, 0, d), (dec_l, dec_b, n_ctx, 256)):
                kr, ki, kl = _hy_filter_call(L, consts[L]["z"], filt, consts[L])
                geom = dict(L=L, nseq=nseq, row_off=off, td=min(td, d))
                z1 = _hy_stage_call(p3, 0, p3, 1, cw[0], cb[0], cw[1], cb[1], kr[0], ki[0], kl[0], skip[0],
                                    consts[L], conv_u=True, out=z1, **geom)
                z = _hy_stage_call(z1, 0, p3, 2, cw[0], cb[0], cw[2], cb[2], kr[1], ki[1], kl[1], skip[1],
                                   consts[L], conv_u=False, out=z, **geom)
            x = _dense_call(lay, z, hy_w_out[j].astype(BF16), resid=(x, g1), name="hy_out")
        wr = jnp.zeros((d, LANES), F32).at[:, :N_EXPERTS].set(moe_w_router[l])
        wr = wr.at[:, N_EXPERTS:N_EXPERTS + N_GROUPS].set(moe_w_group[l])
        br = jnp.zeros((1, LANES), F32).at[0, :N_EXPERTS].set(moe_b_router[l])
        br = br.at[0, N_EXPERTS:N_EXPERTS + N_GROUPS].set(moe_b_group[l])
        x = _moe_layer(lay, x, (norm_ffn[l][None, :], sc2, sh2), g2, wr, br, moe_w_gate[l], moe_w_up[l],
                       moe_w_down[l])

    y_prompt = x[:n_ctx].reshape(batch, seq, d)
    y_sample = x[n_ctx:].reshape(dec_b, dec_l, d)
    return (y_prompt, y_sample, jnp.stack(ckv_out, axis=1), jnp.stack(krope_out, axis=1),
            jnp.stack(nk_out, axis=1), jnp.stack(nv_out, axis=1))
```

```python
import functools
import math

import jax
import jax.numpy as jnp
import numpy as np
from jax import lax
from jax.experimental import pallas as pl
from jax.experimental.pallas import tpu as pltpu

F32 = jnp.float32
BF16 = jnp.bfloat16
HIGHEST = lax.Precision.HIGHEST

D_MODEL = 1024
DEPTH = 4
GRID_W = 64
GRID_SHIFT = 6
MLA_HEADS = 8
Q_RANK = 384
KV_RANK = 128
NOPE_DIM = 64
ROPE_DIM = 32
V_DIM = 64
QK_DIM = NOPE_DIM + ROPE_DIM
ROPE_THETA = 10000.0
NA_HEADS = 8
NA_HEAD_DIM = 64
NA_KH = 8
NA_KW = 16
NA_WIDTH = NA_HEADS * NA_HEAD_DIM
HY_ORDER = 2
HY_BANDS = 8
HY_EMB = 1 + 2 * HY_BANDS
HY_FF = 64
HY_TARGET = 1e-2
HY_FAST_DECAY = 0.3
HY_SLOW_DECAY = 1.5
HY_MOD_SHIFT = 0.05
N_GROUPS = 4
EXPERTS_PER_GROUP = 8
N_EXPERTS = N_GROUPS * EXPERTS_PER_GROUP
D_EXPERT = 256
EPS = 1e-6
NEG_INF = -1e30

LANES = 128
HEAD_PAD = LANES
MOD_ROWS = 8
VMEM_LIMIT = 56 * 1024 * 1024

MLA_SCALE = QK_DIM ** -0.5
NA_SCALE = NA_HEAD_DIM ** -0.5

NA_QROWS = 8
MOE_TILE = 256
TOK_TILE = 256
DFT_ROWS = 512


def _params(n_axes, vmem=VMEM_LIMIT):
    return pltpu.CompilerParams(dimension_semantics=("arbitrary",) * n_axes, vmem_limit_bytes=vmem)


def _mm(a, b):
    return jnp.dot(a.astype(BF16), b.astype(BF16), preferred_element_type=F32)


def _mm_nt(a, b):
    return lax.dot_general(a.astype(BF16), b.astype(BF16), (((1,), (1,)), ((), ())),
                           preferred_element_type=F32)


def _mm_hi(a, b):
    return jnp.dot(a, b, preferred_element_type=F32, precision=HIGHEST)


def _norm_mod(x, g, sc, sh):
    ms = jnp.mean(x * x, axis=-1, keepdims=True)
    return (x * lax.rsqrt(ms + EPS) * g) * (1.0 + sc) + sh


def _lane_iota(shape):
    return lax.broadcasted_iota(jnp.int32, shape, len(shape) - 1)


class _Layout:
    def __init__(self, batch, seq, dec_b, dec_l, past):
        self.batch, self.seq, self.dec_b, self.dec_l, self.past = batch, seq, dec_b, dec_l, past
        self.n_ctx = batch * seq
        self.n_dec = dec_b * dec_l
        self.n = self.n_ctx + self.n_dec
        self.tm = min(TOK_TILE, seq, dec_l)
        assert self.n_ctx % self.tm == 0 and dec_l % self.tm == 0
        assert self.n_ctx % dec_l == 0 and dec_l % GRID_W == 0
        assert 1 + dec_b <= MOD_ROWS

    def seg(self, i):
        a = self.n_ctx // self.tm
        b = self.dec_l // self.tm
        return jnp.where(i < a, 0, 1 + (i - a) // b)

    def mod_spec(self):
        return pl.BlockSpec((None, 1, D_MODEL), lambda i: (self.seg(i), 0, 0))


def _full(shape):
    nd = len(shape)
    return pl.BlockSpec(shape, lambda *_: (0,) * nd)


def _ada_kernel(ct_ref, w_ref, b_ref, o_ref, *, rows):
    c = ct_ref[...]
    s = c * jax.nn.sigmoid(c)
    w = w_ref[...]
    out = [jnp.sum(s[:, r:r + 1] * w, axis=0, keepdims=True) for r in range(rows)]
    out += [jnp.zeros_like(out[0])] * (MOD_ROWS - rows)
    o_ref[...] = jnp.concatenate(out, axis=0) + b_ref[...]


def _ada_call(cvec_t, ada_w, ada_b, rows):
    depth, d, n6 = ada_w.shape
    tn = 1536
    return pl.pallas_call(
        functools.partial(_ada_kernel, rows=rows),
        grid=(depth, n6 // tn),
        in_specs=[
            pl.BlockSpec((d, MOD_ROWS), lambda l, j: (0, 0)),
            pl.BlockSpec((None, d, tn), lambda l, j: (l, 0, j)),
            pl.BlockSpec((None, 1, tn), lambda l, j: (l, 0, j)),
        ],
        out_specs=pl.BlockSpec((None, MOD_ROWS, tn), lambda l, j: (l, 0, j)),
        out_shape=jax.ShapeDtypeStruct((depth, MOD_ROWS, n6), F32),
        compiler_params=_params(2),
        name="ada_mod",
    )(cvec_t, ada_w, ada_b.reshape(depth, 1, n6))


def _dense_kernel(*refs, has_norm, has_bias, has_resid):
    it = iter(refs)
    x_ref = next(it)
    if has_norm:
        g_ref, sc_ref, sh_ref = next(it), next(it), next(it)
    w_ref = next(it)
    b_ref = next(it) if has_bias else None
    if has_resid:
        r_ref, gate_ref = next(it), next(it)
    o_ref = next(it)
    x = x_ref[...]
    if has_norm:
        x = _norm_mod(x.astype(F32), g_ref[...], sc_ref[...], sh_ref[...])
    y = _mm(x, w_ref[...])
    if has_bias:
        y = y + b_ref[...]
    if has_resid:
        y = r_ref[...] + gate_ref[...] * y
    o_ref[...] = y.astype(o_ref.dtype)


def _dense_call(lay, x, w, *, norm=None, bias=None, resid=None, out_dtype=F32, name):
    n, k = x.shape
    m = w.shape[1]
    tm = lay.tm
    args, specs = [x], [pl.BlockSpec((tm, k), lambda i: (i, 0))]
    if norm is not None:
        g, sc, sh = norm
        args += [g, sc, sh]
        specs += [_full((1, k)), lay.mod_spec(), lay.mod_spec()]
    args.append(w)
    specs.append(_full((k, m)))
    if bias is not None:
        args.append(bias)
        specs.append(_full((1, m)))
    if resid is not None:
        r, gate = resid
        args += [r, gate]
        specs += [pl.BlockSpec((tm, m), lambda i: (i, 0)), lay.mod_spec()]
    return pl.pallas_call(
        functools.partial(_dense_kernel, has_norm=norm is not None, has_bias=bias is not None,
                          has_resid=resid is not None),
        grid=(n // tm,),
        in_specs=specs,
        out_specs=pl.BlockSpec((tm, m), lambda i: (i, 0)),
        out_shape=jax.ShapeDtypeStruct((n, m), out_dtype),
        compiler_params=_params(1),
        name=name,
    )(*args)


def _head_rms(xh, gain):
    ss = jnp.sum(xh * xh, axis=-1, keepdims=True) * (1.0 / QK_DIM)
    return xh * lax.rsqrt(ss + EPS) * gain


def _rope(xh, c, s_lo, s_hi):
    quarter = ROPE_DIM // 4
    return xh * c + pltpu.roll(xh, HEAD_PAD - quarter, 1) * s_lo + pltpu.roll(xh, quarter, 1) * s_hi


def _pair_rms(x, gain):
    lo = _lane_iota(x.shape) < NA_HEAD_DIM
    x2 = x * x
    s_lo = jnp.sum(jnp.where(lo, x2, 0.0), axis=-1, keepdims=True) * (1.0 / NA_HEAD_DIM)
    s_hi = jnp.sum(jnp.where(lo, 0.0, x2), axis=-1, keepdims=True) * (1.0 / NA_HEAD_DIM)
    r = jnp.where(lo, lax.rsqrt(s_lo + EPS), lax.rsqrt(s_hi + EPS))
    return x * r * gain


def _attn_in_kernel(x_ref, g_ref, sc_ref, sh_ref, wql_ref, wkvl_ref, wkr_ref, wna_ref, qn_ref, kvn_ref,
                    wqu_ref, wku_ref, wvu_ref, qg_ref, kg_ref, nqg_ref, nkg_ref, rc_ref, rlo_ref, rhi_ref,
                    q_ref, k_ref, v_ref, nq_ref, nk_ref, nv_ref, ckv_ref, krp_ref, *, ctx_tiles):
    latent = (pl.program_id(0) >= ctx_tiles).astype(F32)
    rc = 1.0 + latent * (rc_ref[...] - 1.0)
    rlo = latent * rlo_ref[...]
    rhi = latent * rhi_ref[...]

    h = _norm_mod(x_ref[...], g_ref[...], sc_ref[...], sh_ref[...]).astype(BF16)
    q_lat = _mm(h, wql_ref[...])
    kv_lat = _mm(h, wkvl_ref[...])
    krp = _mm(h, wkr_ref[...])
    na = _mm(h, wna_ref[...])

    def rms(x, g):
        return x * lax.rsqrt(jnp.mean(x * x, axis=-1, keepdims=True) + EPS) * g

    ckv = rms(kv_lat, kvn_ref[...])
    ckv_ref[...] = ckv
    krp_ref[...] = krp
    q_up = _mm(rms(q_lat, qn_ref[...]), wqu_ref[...])
    k_up = _mm(ckv, wku_ref[...])
    v_ref[...] = _mm(ckv, wvu_ref[...]).astype(v_ref.dtype)
    qg, kg = qg_ref[...], kg_ref[...]
    for hd in range(MLA_HEADS):
        sl = slice(hd * HEAD_PAD, (hd + 1) * HEAD_PAD)
        q_ref[:, sl] = _rope(_head_rms(q_up[:, sl], qg), rc, rlo, rhi).astype(q_ref.dtype)
        k_ref[:, sl] = _rope(_head_rms(k_up[:, sl] + krp, kg), rc, rlo, rhi).astype(k_ref.dtype)
    nqg, nkg = nqg_ref[...], nkg_ref[...]
    for j in range(NA_WIDTH // LANES):
        sl = slice(j * LANES, (j + 1) * LANES)
        nq_ref[:, sl] = _pair_rms(na[:, j * LANES:(j + 1) * LANES], nqg).astype(nq_ref.dtype)
        nk_ref[:, sl] = _pair_rms(na[:, NA_WIDTH + j * LANES:NA_WIDTH + (j + 1) * LANES], nkg)
    nv_ref[...] = na[:, 2 * NA_WIDTH:]


def _attn_in_call(lay, x, norm, w, rope):
    n, d = x.shape
    tm = lay.tm
    ctx_tiles = lay.n_ctx // tm
    dec_tiles = lay.dec_l // tm
    g, sc, sh = norm

    def rope_map(i):
        return (jnp.where(i < ctx_tiles, 0, (i - ctx_tiles) % dec_tiles), 0)

    tok = lambda width: pl.BlockSpec((tm, width), lambda i: (i, 0))
    hp = MLA_HEADS * HEAD_PAD
    in_specs = [tok(d), _full((1, d)), lay.mod_spec(), lay.mod_spec(),
                _full(w["wql"].shape), _full(w["wkvl"].shape), _full(w["wkr"].shape), _full(w["wna"].shape),
                _full((1, Q_RANK)), _full((1, KV_RANK)),
                _full(w["wqu"].shape), _full(w["wku"].shape), _full(w["wvu"].shape),
                _full((1, HEAD_PAD)), _full((1, HEAD_PAD)), _full((1, LANES)), _full((1, LANES)),
                pl.BlockSpec((tm, HEAD_PAD), rope_map), pl.BlockSpec((tm, HEAD_PAD), rope_map),
                pl.BlockSpec((tm, HEAD_PAD), rope_map)]
    out_shapes = [jax.ShapeDtypeStruct((n, hp), BF16), jax.ShapeDtypeStruct((n, hp), BF16),
                  jax.ShapeDtypeStruct((n, MLA_HEADS * V_DIM), BF16), jax.ShapeDtypeStruct((n, NA_WIDTH), BF16),
                  jax.ShapeDtypeStruct((n, NA_WIDTH), F32), jax.ShapeDtypeStruct((n, NA_WIDTH), F32),
                  jax.ShapeDtypeStruct((n, KV_RANK), F32), jax.ShapeDtypeStruct((n, HEAD_PAD), F32)]
    out_specs = [tok(s.shape[1]) for s in out_shapes]
    return pl.pallas_call(
        functools.partial(_attn_in_kernel, ctx_tiles=ctx_tiles),
        grid=(n // tm,),
        in_specs=in_specs,
        out_specs=out_specs,
        out_shape=out_shapes,
        compiler_params=_params(1),
        name="attn_in",
    )(x, g, sc, sh, w["wql"], w["wkvl"], w["wkr"], w["wna"], w["qn"], w["kvn"], w["wqu"], w["wku"], w["wvu"],
      w["qg"], w["kg"], w["nqg"], w["nkg"], *rope)


def _ctx_expand_kernel(ckv_ref, krp_ref, wku_ref, wvu_ref, kg_ref, k_ref, v_ref):
    ckv = ckv_ref[...]
    krp = krp_ref[...]
    k_up = _mm(ckv, wku_ref[...])
    v_ref[...] = _mm(ckv, wvu_ref[...]).astype(v_ref.dtype)
    kg = kg_ref[...]
    for hd in range(MLA_HEADS):
        sl = slice(hd * HEAD_PAD, (hd + 1) * HEAD_PAD)
        k_ref[:, sl] = _head_rms(k_up[:, sl] + krp, kg).astype(k_ref.dtype)


def _ctx_expand_call(ckv, krp, w):
    n = ckv.shape[0]
    tm = min(512, n)
    hp = MLA_HEADS * HEAD_PAD
    return pl.pallas_call(
        _ctx_expand_kernel,
        grid=(n // tm,),
        in_specs=[pl.BlockSpec((tm, KV_RANK), lambda i: (i, 0)), pl.BlockSpec((tm, HEAD_PAD), lambda i: (i, 0)),
                  _full(w["wku"].shape), _full(w["wvu"].shape), _full((1, HEAD_PAD))],
        out_specs=[pl.BlockSpec((tm, hp), lambda i: (i, 0)), pl.BlockSpec((tm, MLA_HEADS * V_DIM), lambda i: (i, 0))],
        out_shape=[jax.ShapeDtypeStruct((n, hp), BF16), jax.ShapeDtypeStruct((n, MLA_HEADS * V_DIM), BF16)],
        compiler_params=_params(1),
        name="ctx_expand",
    )(ckv, krp, w["wku"], w["wvu"], w["kg"])


def _softmax_pv(parts):
    m = functools.reduce(jnp.maximum, [jnp.max(s, axis=-1, keepdims=True) for s, _ in parts])
    acc, den = None, None
    for s, v in parts:
        e = jnp.exp(s - m)
        l = jnp.sum(e, axis=-1, keepdims=True)
        o = _mm(e, v)
        acc = o if acc is None else acc + o
        den = l if den is None else den + l
    return acc / den


def _ctx_attn_kernel(q_ref, k_ref, v_ref, nq_ref, nk_ref, nv_ref, o_ref):
    lo = _lane_iota((q_ref.shape[0], LANES)) < V_DIM
    for j in range(MLA_HEADS // 2):
        vp = v_ref[:, j * LANES:(j + 1) * LANES]
        outs = []
        for hd in (2 * j, 2 * j + 1):
            sl = slice(hd * HEAD_PAD, (hd + 1) * HEAD_PAD)
            s = _mm_nt(q_ref[:, sl], k_ref[:, sl])
            outs.append(_softmax_pv([(s, vp)]))
        o_ref[:, j * LANES:(j + 1) * LANES] = jnp.where(lo, outs[0], outs[1]).astype(o_ref.dtype)
    base = MLA_HEADS * V_DIM
    for j in range(NA_HEADS // 2):
        sl = slice(j * LANES, (j + 1) * LANES)
        qp = nq_ref[:, sl]
        kp = nk_ref[:, sl].astype(BF16)
        vp = nv_ref[:, sl].astype(BF16)
        outs = []
        for half in (0, 1):
            qm = jnp.where(lo if half == 0 else jnp.logical_not(lo), qp, jnp.zeros_like(qp))
            s = _mm_nt(qm, kp)
            outs.append(_softmax_pv([(s, vp)]))
        o_ref[:, base + j * LANES:base + (j + 1) * LANES] = jnp.where(lo, outs[0], outs[1]).astype(o_ref.dtype)


def _ctx_attn_call(lay, q, k, v, nq, nk, nv, out):
    s = lay.seq
    blk = lambda width: pl.BlockSpec((s, width), lambda b: (b, 0))
    return pl.pallas_call(
        _ctx_attn_body,
        grid=(lay.batch,),
        in_specs=[blk(q.shape[1]), blk(k.shape[1]), blk(v.shape[1]), blk(nq.shape[1]), blk(nk.shape[1]),
                  blk(nv.shape[1]), pl.BlockSpec(memory_space=pl.ANY)],
        out_specs=blk(out.shape[1]),
        out_shape=jax.ShapeDtypeStruct(out.shape, out.dtype),
        input_output_aliases={6: 0},
        compiler_params=_params(1),
        name="ctx_attn",
    )(q, k, v, nq, nk, nv, out)


def _ctx_attn_body(q_ref, k_ref, v_ref, nq_ref, nk_ref, nv_ref, alias_ref, o_ref):
    del alias_ref
    _ctx_attn_kernel(q_ref, k_ref, v_ref, nq_ref, nk_ref, nv_ref, o_ref)


def _mla_attn_kernel(q_ref, k_ref, v_ref, kc_ref, vc_ref, alias_ref, o_ref):
    del alias_ref
    lo = _lane_iota((q_ref.shape[0], LANES)) < V_DIM
    for j in range(MLA_HEADS // 2):
        vp = v_ref[:, j * LANES:(j + 1) * LANES]
        vcp = vc_ref[:, j * LANES:(j + 1) * LANES]
        outs = []
        for hd in (2 * j, 2 * j + 1):
            sl = slice(hd * HEAD_PAD, (hd + 1) * HEAD_PAD)
            qh = q_ref[:, sl]
            s1 = _mm_nt(qh, k_ref[:, sl])
            s2 = _mm_nt(qh, kc_ref[:, sl])
            outs.append(_softmax_pv([(s1, vp), (s2, vcp)]))
        o_ref[:, j * LANES:(j + 1) * LANES] = jnp.where(lo, outs[0], outs[1]).astype(o_ref.dtype)


def _mla_attn_call(lay, q, k, v, kc, vc, out):
    L, P = lay.dec_l, lay.past
    tq = min(256, L)
    qb = L // tq
    off_q = lay.n_ctx // tq
    off_l = lay.n_ctx // L
    hp = MLA_HEADS * HEAD_PAD
    vw = MLA_HEADS * V_DIM
    return pl.pallas_call(
        _mla_attn_kernel,
        grid=(lay.dec_b, qb),
        in_specs=[pl.BlockSpec((tq, hp), lambda b, i: (off_q + b * qb + i, 0)),
                  pl.BlockSpec((L, hp), lambda b, i: (off_l + b, 0)),
                  pl.BlockSpec((L, vw), lambda b, i: (off_l + b, 0)),
                  pl.BlockSpec((P, hp), lambda b, i: (b, 0)),
                  pl.BlockSpec((P, vw), lambda b, i: (b, 0)),
                  pl.BlockSpec(memory_space=pl.ANY)],
        out_specs=pl.BlockSpec((tq, vw), lambda b, i: (off_q + b * qb + i, 0)),
        out_shape=jax.ShapeDtypeStruct(out.shape, out.dtype),
        input_output_aliases={5: 0},
        compiler_params=_params(2),
        name="mla_attn",
    )(q, k, v, kc, vc, out)


def _na_bias_kernel(rpb_ref, o_ref):
    h = pl.program_id(0)
    n_dy, n_dx = 2 * NA_KH - 1, 2 * NA_KW - 1
    qc = lax.broadcasted_iota(jnp.int32, (GRID_W, LANES), 0)
    kc = _lane_iota((GRID_W, LANES)) & (GRID_W - 1)
    dx = jnp.clip(kc - qc, -(NA_KW - 1), NA_KW - 1) + NA_KW - 1
    cs = jnp.clip(qc - NA_KW // 2, 0, GRID_W - NA_KW)
    col_ok = (kc >= cs) & (kc < cs + NA_KW)
    tiles = []
    for dy in range(n_dy):
        acc = jnp.zeros((GRID_W, LANES), F32)
        for i in range(n_dx):
            acc = jnp.where(dx == i, rpb_ref[(h * n_dy + dy) * n_dx + i], acc)
        tiles.append(jnp.where(col_ok, acc, NEG_INF))
    zero = jnp.zeros((GRID_W, LANES), F32)
    tiles = [zero] + tiles + [zero]
    lo = _lane_iota((GRID_W, LANES)) < GRID_W
    for i in range(n_dy + 1):
        o_ref[i] = jnp.where(lo, tiles[i], tiles[i + 1])


def _na_bias_call(rpb):
    n_pairs = 2 * NA_KH
    return pl.pallas_call(
        _na_bias_kernel,
        grid=(NA_HEADS,),
        in_specs=[pl.BlockSpec(memory_space=pltpu.SMEM)],
        out_specs=pl.BlockSpec((n_pairs, GRID_W, LANES), lambda h: (h, 0, 0)),
        out_shape=jax.ShapeDtypeStruct((NA_HEADS * n_pairs, GRID_W, LANES), F32),
        compiler_params=_params(1),
        name="na_bias",
    )(rpb.reshape(-1))


def _na_attn_kernel(q_ref, k_ref, v_ref, kc_ref, vc_ref, tp_ref, alias_ref, o_ref, *, rows, win):
    del alias_ref
    g = pl.program_id(1)
    n_pairs = 2 * NA_KH
    nq = NA_QROWS * GRID_W
    nk = win * GRID_W
    ws = jnp.clip(NA_QROWS * g - NA_KH // 2, 0, rows - win)
    start = pl.multiple_of(ws * GRID_W, GRID_W)
    kwin = k_ref[pl.ds(start, nk), :].astype(BF16)
    vwin = v_ref[pl.ds(start, nk), :].astype(BF16)
    kctx = kc_ref[...].astype(BF16)
    vctx = vc_ref[...].astype(BF16)

    qr = NA_QROWS * g + (lax.broadcasted_iota(jnp.int32, (nq, nk), 0) >> GRID_SHIFT)
    kr = ws + (_lane_iota((nq, nk)) >> GRID_SHIFT)
    rs = jnp.clip(qr - NA_KH // 2, 0, rows - NA_KH)
    row_ok = (kr >= rs) & (kr < rs + NA_KH)

    lo = _lane_iota((nq, LANES)) < NA_HEAD_DIM
    for j in range(NA_HEADS // 2):
        sl = slice(j * LANES, (j + 1) * LANES)
        qp = q_ref[:, sl]
        kp, vp, kcp, vcp = kwin[:, sl], vwin[:, sl], kctx[:, sl], vctx[:, sl]
        outs = []
        for half in (0, 1):
            hd = 2 * j + half
            bias_rows = []
            for a in range(NA_QROWS):
                blocks = []
                for p in range(win // 2):
                    dy = ws + 2 * p - (NA_QROWS * g + a) + NA_KH - 1
                    idx = jnp.clip(dy + 1, 0, n_pairs - 1)
                    blocks.append(tp_ref[hd * n_pairs + idx])
                bias_rows.append(jnp.concatenate(blocks, axis=1))
            bias = jnp.concatenate(bias_rows, axis=0)
            qm = jnp.where(lo if half == 0 else jnp.logical_not(lo), qp, jnp.zeros_like(qp))
            s_w = jnp.where(row_ok, _mm_nt(qm, kp) + bias, NEG_INF)
            s_c = _mm_nt(qm, kcp)
            outs.append(_softmax_pv([(s_w, vp), (s_c, vcp)]))
        o_ref[:, sl] = jnp.where(lo, outs[0], outs[1]).astype(o_ref.dtype)


def _na_attn_call(lay, nq, nk, nv, kc, vc, tp, out):
    L, P = lay.dec_l, lay.past
    rows = L // GRID_W
    assert rows % NA_QROWS == 0
    win = min(2 * NA_KH, rows)
    groups = rows // NA_QROWS
    tq = NA_QROWS * GRID_W
    off_q = lay.n_ctx // tq
    off_l = lay.n_ctx // L
    n_out_blk = out.shape[1] // NA_WIDTH
    return pl.pallas_call(
        functools.partial(_na_attn_kernel, rows=rows, win=win),
        grid=(lay.dec_b, groups),
        in_specs=[pl.BlockSpec((tq, NA_WIDTH), lambda b, g: (off_q + b * groups + g, 0)),
                  pl.BlockSpec((L, NA_WIDTH), lambda b, g: (off_l + b, 0)),
                  pl.BlockSpec((L, NA_WIDTH), lambda b, g: (off_l + b, 0)),
                  pl.BlockSpec((P, NA_WIDTH), lambda b, g: (b, 0)),
                  pl.BlockSpec((P, NA_WIDTH), lambda b, g: (b, 0)),
                  _full(tp.shape),
                  pl.BlockSpec(memory_space=pl.ANY)],
        out_specs=pl.BlockSpec((tq, NA_WIDTH), lambda b, g: (off_q + b * groups + g, n_out_blk - 1)),
        out_shape=jax.ShapeDtypeStruct(out.shape, out.dtype),
        input_output_aliases={6: 0},
        compiler_params=_params(2),
        name="na_attn",
    )(nq, nk, nv, kc, vc, tp, out)


def _hy_filter_kernel(z_ref, w1_ref, b1_ref, fr_ref, w2_ref, b2_ref, w3a_ref, w3b_ref, w3c_ref, w3d_ref,
                      dl_ref, t_ref, alt_ref, wf_ref, c_ref, s_ref, kr_ref, ki_ref, kl_ref):
    h = jnp.sin(fr_ref[0:1, :] * (_mm_hi(z_ref[...], w1_ref[...]) + b1_ref[...]))
    h = jnp.sin(fr_ref[1:2, :] * (_mm_hi(h, w2_ref[...]) + b2_ref[...]))
    window = jnp.exp(-t_ref[...] * dl_ref[...]) + HY_MOD_SHIFT
    first = lax.broadcasted_iota(jnp.int32, window.shape, 0) == 0
    alt = alt_ref[...]
    wf = wf_ref[...]
    fwd = (w3a_ref, w3b_ref)
    bwd = (w3c_ref, w3d_ref)
    for o in range(HY_ORDER):
        hf = _mm_hi(h, fwd[o][...]) * window
        hb = jnp.where(first, 0.0, _mm_hi(h, bwd[o][...]) * window)
        norm = jnp.sum(jnp.abs(hf), axis=0, keepdims=True) + jnp.sum(jnp.abs(hb), axis=0, keepdims=True) + EPS
        a = (hf + hb) / norm
        d = (hf - hb) / norm
        kl_ref[o] = jnp.sum(a * alt, axis=0, keepdims=True) * (0.5 / a.shape[0])
        a, d = a.astype(BF16), d.astype(BF16)
        n_rows = a.shape[0]
        ch = min(DFT_ROWS, n_rows)
        for i in range(n_rows // ch):
            rows = slice(i * ch, (i + 1) * ch)
            kr_ref[o, rows, :] = jnp.dot(c_ref[rows, :], a, preferred_element_type=F32) * wf[rows]
            ki_ref[o, rows, :] = -jnp.dot(s_ref[rows, :], d, preferred_element_type=F32) * wf[rows]


def _hy_filter_call(L, z, p, consts):
    d = D_MODEL
    dc = 256
    nc = d // dc
    w3 = p["w3"]
    w3_spec = lambda k: pl.BlockSpec((HY_FF, dc), lambda c: (0, k * nc + c))
    zpad = z.shape[1]
    return pl.pallas_call(
        _hy_filter_kernel,
        grid=(nc,),
        in_specs=[_full((L, zpad)), _full((zpad, HY_FF)), _full((1, HY_FF)), _full((2, HY_FF)),
                  _full((HY_FF, HY_FF)), _full((1, HY_FF)), w3_spec(0), w3_spec(1), w3_spec(2), w3_spec(3),
                  pl.BlockSpec((1, dc), lambda c: (0, c)), _full((L, 1)), _full((L, 1)), _full((L, 1)),
                  _full((L, L)), _full((L, L))],
        out_specs=[pl.BlockSpec((HY_ORDER, L, dc), lambda c: (0, 0, c)),
                   pl.BlockSpec((HY_ORDER, L, dc), lambda c: (0, 0, c)),
                   pl.BlockSpec((HY_ORDER, 1, dc), lambda c: (0, 0, c))],
        out_shape=[jax.ShapeDtypeStruct((HY_ORDER, L, d), F32), jax.ShapeDtypeStruct((HY_ORDER, L, d), F32),
                   jax.ShapeDtypeStruct((HY_ORDER, 1, d), F32)],
        compiler_params=_params(1),
        name="hy_filter",
    )(z, p["w1"], p["b1"], p["freq"], p["w2"], p["b2"], w3, w3, w3, w3,
      consts["deltas"], consts["t"], consts["alt"], consts["wf"], consts["C"], consts["S"])


def _hy_stage_kernel(u_ref, g_ref, cwu_ref, cbu_ref, cwg_ref, cbg_ref, kr_ref, ki_ref, kl_ref, skip_ref, alt_ref,
                     c_ref, s_ref, *out_refs, conv_u):
    o_ref = out_refs[-1]
    L = u_ref.shape[0]
    row = lax.broadcasted_iota(jnp.int32, u_ref.shape, 0)
    first, last = row == 0, row == L - 1

    def short_conv(ref, w_ref, b_ref):
        x = ref[...]
        prev = jnp.where(first, 0.0, pltpu.roll(x, 1, 0))
        nxt = jnp.where(last, 0.0, pltpu.roll(x, L - 1, 0))
        w = w_ref[...]
        return prev * w[0:1] + x * w[1:2] + nxt * w[2:3] + b_ref[...]

    u = short_conv(u_ref, cwu_ref, cbu_ref) if conv_u else u_ref[...]
    gate = short_conv(g_ref, cwg_ref, cbg_ref)
    alt = alt_ref[...]
    ub = u.astype(BF16)
    ul_kl = jnp.sum(u * alt, axis=0, keepdims=True) * kl_ref[...]
    skip = skip_ref[...]
    ch = min(DFT_ROWS, L)
    yr, zi = [], []
    for i in range(L // ch):
        rows = slice(i * ch, (i + 1) * ch)
        ur = jnp.dot(c_ref[rows, :], ub, preferred_element_type=F32)
        us = jnp.dot(s_ref[rows, :], ub, preferred_element_type=F32)
        kr, ki = kr_ref[rows, :], ki_ref[rows, :]
        yr.append((ur * kr + us * ki).astype(BF16))
        zi.append((us * kr - ur * ki).astype(BF16))
    yr = jnp.concatenate(yr, axis=0)
    zi = jnp.concatenate(zi, axis=0)
    for i in range(L // ch):
        rows = slice(i * ch, (i + 1) * ch)
        y = jnp.dot(c_ref[rows, :], yr, preferred_element_type=F32) + jnp.dot(s_ref[rows, :], zi,
                                                                              preferred_element_type=F32)
        y = y + alt[rows] * ul_kl + u[rows] * skip
        o_ref[rows, :] = (gate[rows] * y).astype(o_ref.dtype)


def _hy_stage_call(u, u_blk, gate, g_blk, cwu, cbu, cwg, cbg, kr, ki, kl, skip, consts, *, conv_u, L, nseq,
                   row_off, out, td):
    d = D_MODEL
    nc = d // td
    off = row_off // L
    one = pl.Buffered(1)
    chan = lambda rows: pl.BlockSpec((rows, td), lambda c, b: (0, c))
    args = [u, gate, cwu, cbu, cwg, cbg, kr, ki, kl, skip, consts["alt"], consts["C"], consts["S"]]
    in_specs = [pl.BlockSpec((L, td), lambda c, b: (off + b, u_blk * nc + c)),
                pl.BlockSpec((L, td), lambda c, b: (off + b, g_blk * nc + c)),
                chan(3), chan(1), chan(3), chan(1),
                pl.BlockSpec((L, td), lambda c, b: (0, c), pipeline_mode=one),
                pl.BlockSpec((L, td), lambda c, b: (0, c), pipeline_mode=one),
                chan(1), chan(1),
                _full((L, 1)),
                pl.BlockSpec((L, L), lambda c, b: (0, 0), pipeline_mode=one),
                pl.BlockSpec((L, L), lambda c, b: (0, 0), pipeline_mode=one)]
    aliases = {}
    if not isinstance(out, jax.ShapeDtypeStruct):
        aliases = {len(args): 0}
        args.append(out)
        in_specs.append(pl.BlockSpec(memory_space=pl.ANY))
    return pl.pallas_call(
        functools.partial(_hy_stage_kernel, conv_u=conv_u),
        grid=(nc, nseq),
        in_specs=in_specs,
        out_specs=pl.BlockSpec((L, td), lambda c, b: (off + b, c)),
        out_shape=jax.ShapeDtypeStruct(out.shape, out.dtype),
        input_output_aliases=aliases,
        compiler_params=_params(2),
        name="hy_stage_%d" % L,
    )(*args)


def _dft_consts(L):
    f = jnp.arange(L, dtype=jnp.int32)
    ang = ((f[:, None] * f[None, :]) % (2 * L)).astype(F32) * (math.pi / L)
    t = jnp.linspace(0.0, 1.0, L, dtype=F32)[:, None]
    w = 2.0 * math.pi * jnp.arange(L, dtype=F32)[:, None] / L
    bands = jnp.linspace(1e-4, HY_BANDS - 1, HY_BANDS, dtype=F32)[None, :]
    z = jnp.concatenate([t, jnp.cos(bands * w), -jnp.sin(bands * w)], axis=-1)
    z = jnp.pad(z, ((0, 0), (0, 32 - HY_EMB)))
    deltas = jnp.abs(jnp.linspace(math.log(HY_TARGET) / HY_FAST_DECAY, math.log(HY_TARGET) / HY_SLOW_DECAY,
                                  D_MODEL, dtype=F32))[None, :]
    alt = (1.0 - 2.0 * (f % 2).astype(F32))[:, None]
    wf = jnp.where(f == 0, 0.5 / L, 1.0 / L).astype(F32)[:, None]
    return {"C": jnp.cos(ang).astype(BF16), "S": jnp.sin(ang).astype(BF16), "t": t, "z": z, "deltas": deltas,
            "alt": alt, "wf": wf}


def _route_kernel(x_ref, g_ref, sc_ref, sh_ref, wr_ref, br_ref, h_ref, meta_ref, cnt_ref, base_ref):
    i = pl.program_id(0)
    tm = x_ref.shape[0]

    @pl.when(i == 0)
    def _():
        base_ref[...] = jnp.zeros_like(base_ref)

    h = _norm_mod(x_ref[...], g_ref[...], sc_ref[...], sh_ref[...])
    h_ref[...] = h
    logits = _mm_hi(h, wr_ref[...]) + br_ref[...]
    lane_i = _lane_iota(logits.shape)
    lane = lane_i.astype(F32)
    grp_of_lane = (lane_i >> 3).astype(F32)
    big = 1e6
    is_g = (lane_i >= N_EXPERTS) & (lane_i < N_EXPERTS + N_GROUPS)
    glog = jnp.where(is_g, logits, NEG_INF)
    gmax = jnp.max(glog, axis=-1, keepdims=True)
    g_w = 1.0 / jnp.sum(jnp.exp(glog - gmax), axis=-1, keepdims=True)
    g_i = jnp.min(jnp.where(glog == gmax, lane - N_EXPERTS, big), axis=-1, keepdims=True)
    in_grp = (lane_i < N_EXPERTS) & (grp_of_lane == g_i)
    elog = jnp.where(in_grp, logits, NEG_INF)
    m1 = jnp.max(elog, axis=-1, keepdims=True)
    i1 = jnp.min(jnp.where(elog == m1, lane, big), axis=-1, keepdims=True)
    elog2 = jnp.where(lane == i1, NEG_INF, elog)
    m2 = jnp.max(elog2, axis=-1, keepdims=True)
    i2 = jnp.min(jnp.where(elog2 == m2, lane, big), axis=-1, keepdims=True)
    a2 = jnp.exp(m2 - m1)
    w1 = g_w / (1.0 + a2)
    w2 = g_w * a2 / (1.0 + a2)

    oh1 = jnp.where(lane == i1, 1.0, 0.0).astype(BF16)
    oh2 = jnp.where(lane == i2, 1.0, 0.0).astype(BF16)
    tri = jnp.where(lax.broadcasted_iota(jnp.int32, (tm, tm), 0) > lax.broadcasted_iota(jnp.int32, (tm, tm), 1),
                    1.0, 0.0).astype(BF16)
    cum1 = jnp.dot(tri, oh1, preferred_element_type=F32)
    cum2 = jnp.dot(tri, oh2, preferred_element_type=F32)
    tot1 = jnp.sum(oh1.astype(F32), axis=0, keepdims=True)
    tot2 = jnp.sum(oh2.astype(F32), axis=0, keepdims=True)
    base = base_ref[...]
    r1 = jnp.sum(jnp.where(lane == i1, base + cum1, 0.0), axis=-1, keepdims=True)
    r2 = jnp.sum(jnp.where(lane == i2, base + tot1 + cum2, 0.0), axis=-1, keepdims=True)
    base = base + tot1 + tot2
    base_ref[...] = base
    cnt_ref[...] = base
    cols = [i1, i2, r1, r2, w1, w2]
    meta = jnp.zeros(logits.shape, F32)
    for c, val in enumerate(cols):
        meta = jnp.where(lane_i == c, val, meta)
    meta_ref[...] = meta


def _route_call(lay, x, norm, wr, br):
    n, d = x.shape
    tm = lay.tm
    g, sc, sh = norm
    tok = lambda width: pl.BlockSpec((tm, width), lambda i: (i, 0))
    return pl.pallas_call(
        _route_kernel,
        grid=(n // tm,),
        in_specs=[tok(d), _full((1, d)), lay.mod_spec(), lay.mod_spec(), _full((d, LANES)), _full((1, LANES))],
        out_specs=[tok(d), tok(LANES), _full((1, LANES))],
        out_shape=[jax.ShapeDtypeStruct((n, d), F32), jax.ShapeDtypeStruct((n, LANES), F32),
                   jax.ShapeDtypeStruct((1, LANES), F32)],
        scratch_shapes=[pltpu.VMEM((1, LANES), F32)],
        compiler_params=_params(1),
        name="moe_route",
    )(x, g, sc, sh, wr, br)


def _row_copy(src_ref, s, dst_ref, d, sem):
    return pltpu.make_async_copy(src_ref.at[pl.ds(s, 1)], dst_ref.at[pl.ds(d, 1)], sem)


def _expert_kernel(te_ref, src_ref, nxt_ref, dst_ref, h_ref, wg_ref, wu_ref, wd_ref, o_ref,
                   xbuf, ybuf, zbuf, sem_in, sem_out, sem_z, *, trash_base):
    j = pl.program_id(0)
    tiles = pl.num_programs(0)
    used = te_ref[tiles]
    slot = lax.rem(j, 2)
    tm = xbuf.shape[1]

    def start_gather(idx_ref, s):
        for r in range(tm):
            _row_copy(h_ref, idx_ref[0, r], xbuf.at[s], r, sem_in.at[s]).start(priority=r % 2)

    def wait_gather(s):
        for r in range(tm):
            _row_copy(h_ref, 0, xbuf.at[s], r, sem_in.at[s]).wait()

    def start_scatter(s):
        for r in range(tm):
            _row_copy(ybuf.at[s], r, o_ref, dst_ref[0, r], sem_out.at[s]).start(priority=r % 2)

    def wait_scatter(s):
        for r in range(tm):
            _row_copy(ybuf.at[s], r, o_ref, 0, sem_out.at[s]).wait()

    @pl.when(j == 0)
    def _():
        zbuf[...] = jnp.zeros_like(zbuf)
        fills = [pltpu.make_async_copy(zbuf, o_ref.at[pl.ds(trash_base + k * tm, tm)], sem_z)
                 for k in range(N_EXPERTS)]
        for f in fills:
            f.start()
        start_gather(src_ref, 0)
        for f in fills:
            f.wait()

    @pl.when(j + 1 < used)
    def _():
        start_gather(nxt_ref, 1 - slot)

    @pl.when(j < used)
    def _():
        wait_gather(slot)

        @pl.when(j >= 2)
        def _():
            wait_scatter(slot)

        x = xbuf[slot].astype(BF16)
        hg = _mm(x, wg_ref[...])
        hu = _mm(x, wu_ref[...])
        act = hg * jax.nn.sigmoid(hg) * hu
        ybuf[slot] = _mm(act, wd_ref[...])
        start_scatter(slot)

        @pl.when(j == used - 1)
        def _():
            wait_scatter(slot)

            @pl.when(j >= 1)
            def _():
                wait_scatter(1 - slot)


def _expert_call(layer, tile_expert, src, dst, h, wg, wu, wd):
    n, d = h.shape
    f = wg.shape[-1]
    tiles = src.shape[0]
    tm = MOE_TILE
    idx = lambda fn: pl.BlockSpec((None, 1, tm), fn, memory_space=pltpu.SMEM)
    wspec = lambda a, b: pl.BlockSpec((None, None, a, b), lambda j, te: (layer, te[j], 0, 0))
    grid_spec = pltpu.PrefetchScalarGridSpec(
        num_scalar_prefetch=1,
        grid=(tiles,),
        in_specs=[idx(lambda j, te: (j, 0, 0)),
                  idx(lambda j, te: (jnp.minimum(j + 1, tiles - 1), 0, 0)),
                  idx(lambda j, te: (j, 0, 0)),
                  pl.BlockSpec(memory_space=pl.ANY),
                  wspec(d, f), wspec(d, f), wspec(f, d)],
        out_specs=pl.BlockSpec(memory_space=pl.ANY),
        scratch_shapes=[pltpu.VMEM((2, tm, d), F32), pltpu.VMEM((2, tm, d), F32), pltpu.VMEM((tm, d), F32),
                        pltpu.SemaphoreType.DMA((2,)), pltpu.SemaphoreType.DMA((2,)), pltpu.SemaphoreType.DMA(())],
    )
    return pl.pallas_call(
        functools.partial(_expert_kernel, trash_base=2 * n),
        grid_spec=grid_spec,
        out_shape=jax.ShapeDtypeStruct((2 * n + N_EXPERTS * tm, d), F32),
        compiler_params=_params(1),
        name="moe_experts",
    )(tile_expert, src, src, dst, h, wg, wu, wd)


def _combine_kernel(x_ref, gate_ref, meta_ref, y0_ref, y1_ref, o_ref):
    meta = meta_ref[...]
    o_ref[...] = x_ref[...] + gate_ref[...] * (meta[:, 4:5] * y0_ref[...] + meta[:, 5:6] * y1_ref[...])


def _combine_call(lay, x, gate, meta, ys):
    n, d = x.shape
    tm = lay.tm
    tok = lambda width: pl.BlockSpec((tm, width), lambda i: (i, 0))
    return pl.pallas_call(
        _combine_kernel,
        grid=(n // tm,),
        in_specs=[tok(d), lay.mod_spec(), tok(LANES), tok(d), pl.BlockSpec((tm, d), lambda i: (n // tm + i, 0))],
        out_specs=tok(d),
        out_shape=jax.ShapeDtypeStruct((n, d), F32),
        compiler_params=_params(1),
        name="moe_combine",
    )(x, gate, meta, ys, ys)


def _moe_layer(lay, layer, x, norm, gate, wr, br, wg, wu, wd):
    n = x.shape[0]
    tm = MOE_TILE
    h, meta, counts = _route_call(lay, x, norm, wr, br)
    cnt = counts[0, :N_EXPERTS].astype(jnp.int32)
    padded = ((cnt + tm - 1) // tm) * tm
    ends = jnp.cumsum(padded)
    starts = ends - padded
    m4 = meta[:, 0:4].T.astype(jnp.int32)
    onehot = m4[0:2, :, None] == jnp.arange(N_EXPERTS, dtype=jnp.int32)[None, None, :]
    pos = jnp.sum(jnp.where(onehot, starts[None, None, :], 0), axis=-1) + m4[2:4]
    rows = 2 * n + N_EXPERTS * tm
    tiles = rows // tm
    val = jnp.arange(2 * n, dtype=jnp.int32)
    inv = jnp.full((rows,), -1, jnp.int32).at[pos.reshape(-1)].set(val, unique_indices=True)
    is_pad = inv < 0
    trash = 2 * n - 1 + jnp.cumsum(is_pad.astype(jnp.int32))
    dst = jnp.where(is_pad, trash, inv).reshape(tiles, 1, tm)
    src = jnp.where(is_pad, 0, jnp.where(inv >= n, inv - n, inv)).reshape(tiles, 1, tm)
    tile_start = jnp.arange(tiles, dtype=jnp.int32) * tm
    tile_expert = jnp.sum((tile_start[:, None] >= ends[None, :]).astype(jnp.int32), axis=1)
    used = ends[-1] // tm
    last = jnp.take(tile_expert, jnp.maximum(used - 1, 0))
    tile_expert = jnp.where(tile_start // tm < used, tile_expert, last)
    tile_expert = jnp.concatenate([tile_expert, used[None]]).astype(jnp.int32)
    ys = _expert_call(layer, tile_expert, src, dst, h, wg, wu, wd)
    return _combine_call(lay, x, gate, meta, ys)


def _attn_weights(a, attn_w_in, mla_q_norm, mla_kv_norm, mla_w_q_up, mla_w_kv_up, mla_q_gain, mla_k_gain,
                  na_q_gain, na_k_gain):
    w_in = attn_w_in[a]
    o1, o2, o3 = Q_RANK, Q_RANK + KV_RANK, Q_RANK + KV_RANK + ROPE_DIM
    pad_head = lambda g: jnp.pad(g, (0, HEAD_PAD - QK_DIM))[None, :]
    wkr = jnp.zeros((D_MODEL, HEAD_PAD), F32).at[:, NOPE_DIM:QK_DIM].set(w_in[:, o2:o3])
    wqu = jnp.pad(mla_w_q_up[a].reshape(Q_RANK, MLA_HEADS, QK_DIM), ((0, 0), (0, 0), (0, HEAD_PAD - QK_DIM)))
    wkv = mla_w_kv_up[a].reshape(KV_RANK, MLA_HEADS, NOPE_DIM + V_DIM)
    wku = jnp.pad(wkv[:, :, :NOPE_DIM], ((0, 0), (0, 0), (0, HEAD_PAD - NOPE_DIM)))
    return {
        "wql": w_in[:, :o1].astype(BF16), "wkvl": w_in[:, o1:o2].astype(BF16), "wkr": wkr.astype(BF16),
        "wna": w_in[:, o3:].astype(BF16),
        "qn": mla_q_norm[a][None, :], "kvn": mla_kv_norm[a][None, :],
        "wqu": wqu.reshape(Q_RANK, MLA_HEADS * HEAD_PAD).astype(BF16),
        "wku": wku.reshape(KV_RANK, MLA_HEADS * HEAD_PAD).astype(BF16),
        "wvu": wkv[:, :, NOPE_DIM:].reshape(KV_RANK, MLA_HEADS * V_DIM).astype(BF16),
        "qg": pad_head(mla_q_gain[a]) * MLA_SCALE, "kg": pad_head(mla_k_gain[a]),
        "nqg": jnp.tile(na_q_gain[a], LANES // NA_HEAD_DIM)[None, :] * NA_SCALE,
        "nkg": jnp.tile(na_k_gain[a], LANES // NA_HEAD_DIM)[None, :],
    }


def _rope_tables(L):
    t = jnp.arange(L)
    quarter = ROPE_DIM // 4
    inv = ROPE_THETA ** (-jnp.arange(quarter, dtype=F32) / quarter)
    ang_r = (t // GRID_W).astype(F32)[:, None] * inv[None, :]
    ang_c = (t % GRID_W).astype(F32)[:, None] * inv[None, :]
    zeros = jnp.zeros((L, quarter), F32)
    ones_nope = jnp.ones((L, NOPE_DIM), F32)
    pad = jnp.zeros((L, HEAD_PAD - QK_DIM), F32)
    cos = jnp.concatenate([ones_nope, jnp.cos(ang_r), jnp.cos(ang_r), jnp.cos(ang_c), jnp.cos(ang_c), pad], axis=1)
    s_lo = jnp.concatenate([0 * ones_nope, -jnp.sin(ang_r), zeros, -jnp.sin(ang_c), zeros, pad], axis=1)
    s_hi = jnp.concatenate([0 * ones_nope, zeros, jnp.sin(ang_r), zeros, jnp.sin(ang_c), pad], axis=1)
    return cos, s_lo, s_hi


def kernel(x_prompt, x_sample, cache_mla_ckv, cache_mla_krope, cache_na_k, cache_na_v, c, c_ctx, ada_w, ada_b, norm_mix, norm_ffn, attn_w_in, mla_q_norm, mla_kv_norm, mla_w_q_up, mla_w_kv_up, mla_q_gain, mla_k_gain, na_q_gain, na_k_gain, na_rpb, attn_w_out, hy_w_in, hy_b_in, hy_conv_w, hy_conv_b, hy_filt_w1, hy_filt_b1, hy_filt_freq, hy_filt_w2, hy_filt_b2, hy_filt_w3, hy_skip, hy_w_out, moe_w_group, moe_b_group, moe_w_router, moe_b_router, moe_w_gate, moe_w_up, moe_w_down):
    batch, seq, d = x_prompt.shape
    dec_b, dec_l, _ = x_sample.shape
    past = cache_mla_ckv.shape[2]
    depth = ada_w.shape[0]
    lay = _Layout(batch, seq, dec_b, dec_l, past)
    n_ctx = lay.n_ctx

    x = jnp.concatenate([x_prompt.reshape(n_ctx, d), x_sample.reshape(lay.n_dec, d)], axis=0)
    cvec = jnp.concatenate([c_ctx[None, :], c], axis=0)
    cvec_t = jnp.pad(cvec, ((0, MOD_ROWS - cvec.shape[0]), (0, 0))).T
    mods = _ada_call(cvec_t, ada_w, ada_b, 1 + dec_b)

    def mod(l, j):
        return mods[l, :, j * d:(j + 1) * d].reshape(MOD_ROWS, 1, d)

    rope = _rope_tables(dec_l)
    consts = {L: _dft_consts(L) for L in sorted({seq, dec_l})}
    ckv_out, krope_out, nk_out, nv_out = [], [], [], []

    for l in range(depth):
        sh1, sc1, g1, sh2, sc2, g2 = [mod(l, j) for j in range(6)]
        norm1 = (norm_mix[l][None, :], sc1, sh1)
        if l % 2 == 0:
            a = l // 2
            w = _attn_weights(a, attn_w_in, mla_q_norm, mla_kv_norm, mla_w_q_up, mla_w_kv_up, mla_q_gain,
                              mla_k_gain, na_q_gain, na_k_gain)
            q, k, v, nq, nk, nv, ckv, krp = _attn_in_call(lay, x, norm1, w, rope)
            ckv_out.append(ckv[:n_ctx].reshape(batch, seq, KV_RANK))
            krope_out.append(krp[:n_ctx, NOPE_DIM:QK_DIM].reshape(batch, seq, ROPE_DIM))
            nk_out.append(nk[:n_ctx].reshape(batch, seq, NA_HEADS, NA_HEAD_DIM))
            nv_out.append(nv[:n_ctx].reshape(batch, seq, NA_HEADS, NA_HEAD_DIM))
            krp_c = jnp.pad(cache_mla_krope[:, a].reshape(dec_b * past, ROPE_DIM),
                            ((0, 0), (NOPE_DIM, HEAD_PAD - QK_DIM)))
            kc, vc = _ctx_expand_call(cache_mla_ckv[:, a].reshape(dec_b * past, KV_RANK), krp_c, w)
            tp = _na_bias_call(na_rpb[a])
            att = jnp.zeros((lay.n, MLA_HEADS * V_DIM + NA_WIDTH), BF16)
            att = _ctx_attn_call(lay, q, k, v, nq, nk, nv, att)
            att = _mla_attn_call(lay, q, k, v, kc, vc, att)
            att = _na_attn_call(lay, nq, nk, nv, cache_na_k[:, a].reshape(dec_b * past, NA_WIDTH),
                                cache_na_v[:, a].reshape(dec_b * past, NA_WIDTH), tp, att)
            x = _dense_call(lay, att, attn_w_out[a].astype(BF16), resid=(x, g1), name="attn_out")
        else:
            j = l // 2
            p3 = _dense_call(lay, x, hy_w_in[j].astype(BF16), norm=norm1, bias=hy_b_in[j][None, :], name="hy_in")
            cw = hy_conv_w[j].reshape(3, 3, d).transpose(1, 0, 2)
            cb = hy_conv_b[j].reshape(3, 1, d)
            skip = hy_skip[j].reshape(HY_ORDER, 1, d)
            filt = {"w1": jnp.pad(hy_filt_w1[j], ((0, 32 - HY_EMB), (0, 0))), "b1": hy_filt_b1[j][None, :],
                    "freq": hy_filt_freq[j], "w2": hy_filt_w2[j], "b2": hy_filt_b2[j][None, :],
                    "w3": hy_filt_w3[j]}
            z1 = jnp.zeros((lay.n, d), F32)
            z = jnp.zeros((lay.n, d), BF16)
            for L, nseq, off, td in ((seq, batch, 0, d), (dec_l, dec_b, n_ctx, 256)):
                kr, ki, kl = _hy_filter_call(L, consts[L]["z"], filt, consts[L])
                geom = dict(L=L, nseq=nseq, row_off=off, td=min(td, d))
                z1 = _hy_stage_call(p3, 0, p3, 1, cw[0], cb[0], cw[1], cb[1], kr[0], ki[0], kl[0], skip[0],
                                    consts[L], conv_u=True, out=z1, **geom)
                z = _hy_stage_call(z1, 0, p3, 2, cw[0], cb[0], cw[2], cb[2], kr[1], ki[1], kl[1], skip[1],
                                   consts[L], conv_u=False, out=z, **geom)
            x = _dense_call(lay, z, hy_w_out[j].astype(BF16), resid=(x, g1), name="hy_out")
        wr = jnp.zeros((d, LANES), F32).at[:, :N_EXPERTS].set(moe_w_router[l])
        wr = wr.at[:, N_EXPERTS:N_EXPERTS + N_GROUPS].set(moe_w_group[l])
        br = jnp.zeros((1, LANES), F32).at[0, :N_EXPERTS].set(moe_b_router[l])
        br = br.at[0, N_EXPERTS:N_EXPERTS + N_GROUPS].set(moe_b_group[l])
        x = _moe_layer(lay, l, x, (norm_ffn[l][None, :], sc2, sh2), g2, wr, br, moe_w_gate, moe_w_up, moe_w_down)

    y_prompt = x[:n_ctx].reshape(batch, seq, d)
    y_sample = x[n_ctx:].reshape(dec_b, dec_l, d)
    return (y_prompt, y_sample, jnp.stack(ckv_out, axis=1), jnp.stack(krope_out, axis=1),
            jnp.stack(nk_out, axis=1), jnp.stack(nv_out, axis=1))
```

```python
import functools
import math

import jax
import jax.numpy as jnp
import numpy as np
from jax import lax
from jax.experimental import pallas as pl
from jax.experimental.pallas import tpu as pltpu

F32 = jnp.float32
BF16 = jnp.bfloat16
HIGHEST = lax.Precision.HIGHEST

D_MODEL = 1024
DEPTH = 4
GRID_W = 64
GRID_SHIFT = 6
MLA_HEADS = 8
Q_RANK = 384
KV_RANK = 128
NOPE_DIM = 64
ROPE_DIM = 32
V_DIM = 64
QK_DIM = NOPE_DIM + ROPE_DIM
ROPE_THETA = 10000.0
NA_HEADS = 8
NA_HEAD_DIM = 64
NA_KH = 8
NA_KW = 16
NA_WIDTH = NA_HEADS * NA_HEAD_DIM
HY_ORDER = 2
HY_BANDS = 8
HY_EMB = 1 + 2 * HY_BANDS
HY_FF = 64
HY_TARGET = 1e-2
HY_FAST_DECAY = 0.3
HY_SLOW_DECAY = 1.5
HY_MOD_SHIFT = 0.05
N_GROUPS = 4
EXPERTS_PER_GROUP = 8
N_EXPERTS = N_GROUPS * EXPERTS_PER_GROUP
D_EXPERT = 256
EPS = 1e-6
NEG_INF = -1e30

LANES = 128
HEAD_PAD = LANES
MOD_ROWS = 8
VMEM_LIMIT = 56 * 1024 * 1024

MLA_SCALE = QK_DIM ** -0.5
NA_SCALE = NA_HEAD_DIM ** -0.5

NA_QROWS = 8
MOE_TILE = 256
TOK_TILE = 256
DFT_ROWS = 512


def _params(n_axes, vmem=VMEM_LIMIT):
    return pltpu.CompilerParams(dimension_semantics=("arbitrary",) * n_axes, vmem_limit_bytes=vmem)


def _mm(a, b):
    return jnp.dot(a.astype(BF16), b.astype(BF16), preferred_element_type=F32)


def _mm_nt(a, b):
    return lax.dot_general(a.astype(BF16), b.astype(BF16), (((1,), (1,)), ((), ())),
                           preferred_element_type=F32)


def _mm_hi(a, b):
    return jnp.dot(a, b, preferred_element_type=F32, precision=HIGHEST)


def _norm_mod(x, g, sc, sh):
    ms = jnp.mean(x * x, axis=-1, keepdims=True)
    return (x * lax.rsqrt(ms + EPS) * g) * (1.0 + sc) + sh


def _lane_iota(shape):
    return lax.broadcasted_iota(jnp.int32, shape, len(shape) - 1)


class _Layout:
    def __init__(self, batch, seq, dec_b, dec_l, past):
        self.batch, self.seq, self.dec_b, self.dec_l, self.past = batch, seq, dec_b, dec_l, past
        self.n_ctx = batch * seq
        self.n_dec = dec_b * dec_l
        self.n = self.n_ctx + self.n_dec
        self.tm = min(TOK_TILE, seq, dec_l)
        assert self.n_ctx % self.tm == 0 and dec_l % self.tm == 0
        assert self.n_ctx % dec_l == 0 and dec_l % GRID_W == 0
        assert 1 + dec_b <= MOD_ROWS

    def seg(self, i):
        a = self.n_ctx // self.tm
        b = self.dec_l // self.tm
        return jnp.where(i < a, 0, 1 + (i - a) // b)

    def mod_spec(self):
        return pl.BlockSpec((None, 1, D_MODEL), lambda i: (self.seg(i), 0, 0))


def _full(shape):
    nd = len(shape)
    return pl.BlockSpec(shape, lambda *_: (0,) * nd)


def _ada_kernel(ct_ref, w_ref, b_ref, o_ref, *, rows):
    c = ct_ref[...]
    s = c * jax.nn.sigmoid(c)
    w = w_ref[...]
    out = [jnp.sum(s[:, r:r + 1] * w, axis=0, keepdims=True) for r in range(rows)]
    out += [jnp.zeros_like(out[0])] * (MOD_ROWS - rows)
    o_ref[...] = jnp.concatenate(out, axis=0) + b_ref[...]


def _ada_call(cvec_t, ada_w, ada_b, rows):
    depth, d, n6 = ada_w.shape
    tn = 1536
    return pl.pallas_call(
        functools.partial(_ada_kernel, rows=rows),
        grid=(depth, n6 // tn),
        in_specs=[
            pl.BlockSpec((d, MOD_ROWS), lambda l, j: (0, 0)),
            pl.BlockSpec((None, d, tn), lambda l, j: (l, 0, j)),
            pl.BlockSpec((None, 1, tn), lambda l, j: (l, 0, j)),
        ],
        out_specs=pl.BlockSpec((None, MOD_ROWS, tn), lambda l, j: (l, 0, j)),
        out_shape=jax.ShapeDtypeStruct((depth, MOD_ROWS, n6), F32),
        compiler_params=_params(2),
        name="ada_mod",
    )(cvec_t, ada_w, ada_b.reshape(depth, 1, n6))


def _dense_kernel(*refs, has_norm, has_bias, has_resid):
    it = iter(refs)
    x_ref = next(it)
    if has_norm:
        g_ref, sc_ref, sh_ref = next(it), next(it), next(it)
    w_ref = next(it)
    b_ref = next(it) if has_bias else None
    if has_resid:
        r_ref, gate_ref = next(it), next(it)
    o_ref = next(it)
    x = x_ref[...]
    if has_norm:
        x = _norm_mod(x.astype(F32), g_ref[...], sc_ref[...], sh_ref[...])
    y = _mm(x, w_ref[...])
    if has_bias:
        y = y + b_ref[...]
    if has_resid:
        y = r_ref[...] + gate_ref[...] * y
    o_ref[...] = y.astype(o_ref.dtype)


def _dense_call(lay, x, w, *, norm=None, bias=None, resid=None, out_dtype=F32, name):
    n, k = x.shape
    m = w.shape[1]
    tm = lay.tm
    args, specs = [x], [pl.BlockSpec((tm, k), lambda i: (i, 0))]
    if norm is not None:
        g, sc, sh = norm
        args += [g, sc, sh]
        specs += [_full((1, k)), lay.mod_spec(), lay.mod_spec()]
    args.append(w)
    specs.append(_full((k, m)))
    if bias is not None:
        args.append(bias)
        specs.append(_full((1, m)))
    if resid is not None:
        r, gate = resid
        args += [r, gate]
        specs += [pl.BlockSpec((tm, m), lambda i: (i, 0)), lay.mod_spec()]
    return pl.pallas_call(
        functools.partial(_dense_kernel, has_norm=norm is not None, has_bias=bias is not None,
                          has_resid=resid is not None),
        grid=(n // tm,),
        in_specs=specs,
        out_specs=pl.BlockSpec((tm, m), lambda i: (i, 0)),
        out_shape=jax.ShapeDtypeStruct((n, m), out_dtype),
        compiler_params=_params(1),
        name=name,
    )(*args)


def _head_rms(xh, gain):
    ss = jnp.sum(xh * xh, axis=-1, keepdims=True) * (1.0 / QK_DIM)
    return xh * lax.rsqrt(ss + EPS) * gain


def _rope(xh, c, s_lo, s_hi):
    quarter = ROPE_DIM // 4
    return xh * c + pltpu.roll(xh, HEAD_PAD - quarter, 1) * s_lo + pltpu.roll(xh, quarter, 1) * s_hi


def _pair_rms(x, gain):
    lo = _lane_iota(x.shape) < NA_HEAD_DIM
    x2 = x * x
    s_lo = jnp.sum(jnp.where(lo, x2, 0.0), axis=-1, keepdims=True) * (1.0 / NA_HEAD_DIM)
    s_hi = jnp.sum(jnp.where(lo, 0.0, x2), axis=-1, keepdims=True) * (1.0 / NA_HEAD_DIM)
    r = jnp.where(lo, lax.rsqrt(s_lo + EPS), lax.rsqrt(s_hi + EPS))
    return x * r * gain


def _attn_in_kernel(x_ref, g_ref, sc_ref, sh_ref, wql_ref, wkvl_ref, wkr_ref, wna_ref, qn_ref, kvn_ref,
                    wqu_ref, wku_ref, wvu_ref, qg_ref, kg_ref, nqg_ref, nkg_ref, rc_ref, rlo_ref, rhi_ref,
                    q_ref, k_ref, v_ref, nq_ref, nk_ref, nv_ref, ckv_ref, krp_ref, *, ctx_tiles):
    latent = (pl.program_id(0) >= ctx_tiles).astype(F32)
    rc = 1.0 + latent * (rc_ref[...] - 1.0)
    rlo = latent * rlo_ref[...]
    rhi = latent * rhi_ref[...]

    h = _norm_mod(x_ref[...], g_ref[...], sc_ref[...], sh_ref[...]).astype(BF16)
    q_lat = _mm(h, wql_ref[...])
    kv_lat = _mm(h, wkvl_ref[...])
    krp = _mm(h, wkr_ref[...])
    na = _mm(h, wna_ref[...])

    def rms(x, g):
        return x * lax.rsqrt(jnp.mean(x * x, axis=-1, keepdims=True) + EPS) * g

    ckv = rms(kv_lat, kvn_ref[...])
    ckv_ref[...] = ckv
    krp_ref[...] = krp
    q_up = _mm(rms(q_lat, qn_ref[...]), wqu_ref[...])
    k_up = _mm(ckv, wku_ref[...])
    v_ref[...] = _mm(ckv, wvu_ref[...]).astype(v_ref.dtype)
    qg, kg = qg_ref[...], kg_ref[...]
    for hd in range(MLA_HEADS):
        sl = slice(hd * HEAD_PAD, (hd + 1) * HEAD_PAD)
        q_ref[:, sl] = _rope(_head_rms(q_up[:, sl], qg), rc, rlo, rhi).astype(q_ref.dtype)
        k_ref[:, sl] = _rope(_head_rms(k_up[:, sl] + krp, kg), rc, rlo, rhi).astype(k_ref.dtype)
    nqg, nkg = nqg_ref[...], nkg_ref[...]
    for j in range(NA_WIDTH // LANES):
        sl = slice(j * LANES, (j + 1) * LANES)
        nq_ref[:, sl] = _pair_rms(na[:, j * LANES:(j + 1) * LANES], nqg).astype(nq_ref.dtype)
        nk_ref[:, sl] = _pair_rms(na[:, NA_WIDTH + j * LANES:NA_WIDTH + (j + 1) * LANES], nkg)
    nv_ref[...] = na[:, 2 * NA_WIDTH:]


def _attn_in_call(lay, x, norm, w, rope):
    n, d = x.shape
    tm = lay.tm
    ctx_tiles = lay.n_ctx // tm
    dec_tiles = lay.dec_l // tm
    g, sc, sh = norm

    def rope_map(i):
        return (jnp.where(i < ctx_tiles, 0, (i - ctx_tiles) % dec_tiles), 0)

    tok = lambda width: pl.BlockSpec((tm, width), lambda i: (i, 0))
    hp = MLA_HEADS * HEAD_PAD
    in_specs = [tok(d), _full((1, d)), lay.mod_spec(), lay.mod_spec(),
                _full(w["wql"].shape), _full(w["wkvl"].shape), _full(w["wkr"].shape), _full(w["wna"].shape),
                _full((1, Q_RANK)), _full((1, KV_RANK)),
                _full(w["wqu"].shape), _full(w["wku"].shape), _full(w["wvu"].shape),
                _full((1, HEAD_PAD)), _full((1, HEAD_PAD)), _full((1, LANES)), _full((1, LANES)),
                pl.BlockSpec((tm, HEAD_PAD), rope_map), pl.BlockSpec((tm, HEAD_PAD), rope_map),
                pl.BlockSpec((tm, HEAD_PAD), rope_map)]
    out_shapes = [jax.ShapeDtypeStruct((n, hp), BF16), jax.ShapeDtypeStruct((n, hp), BF16),
                  jax.ShapeDtypeStruct((n, MLA_HEADS * V_DIM), BF16), jax.ShapeDtypeStruct((n, NA_WIDTH), BF16),
                  jax.ShapeDtypeStruct((n, NA_WIDTH), F32), jax.ShapeDtypeStruct((n, NA_WIDTH), F32),
                  jax.ShapeDtypeStruct((n, KV_RANK), F32), jax.ShapeDtypeStruct((n, HEAD_PAD), F32)]
    out_specs = [tok(s.shape[1]) for s in out_shapes]
    return pl.pallas_call(
        functools.partial(_attn_in_kernel, ctx_tiles=ctx_tiles),
        grid=(n // tm,),
        in_specs=in_specs,
        out_specs=out_specs,
        out_shape=out_shapes,
        compiler_params=_params(1),
        name="attn_in",
    )(x, g, sc, sh, w["wql"], w["wkvl"], w["wkr"], w["wna"], w["qn"], w["kvn"], w["wqu"], w["wku"], w["wvu"],
      w["qg"], w["kg"], w["nqg"], w["nkg"], *rope)


def _ctx_expand_kernel(ckv_ref, krp_ref, wku_ref, wvu_ref, kg_ref, k_ref, v_ref):
    ckv = ckv_ref[...]
    krp = krp_ref[...]
    k_up = _mm(ckv, wku_ref[...])
    v_ref[...] = _mm(ckv, wvu_ref[...]).astype(v_ref.dtype)
    kg = kg_ref[...]
    for hd in range(MLA_HEADS):
        sl = slice(hd * HEAD_PAD, (hd + 1) * HEAD_PAD)
        k_ref[:, sl] = _head_rms(k_up[:, sl] + krp, kg).astype(k_ref.dtype)


def _ctx_expand_call(ckv, krp, w):
    n = ckv.shape[0]
    tm = min(512, n)
    hp = MLA_HEADS * HEAD_PAD
    return pl.pallas_call(
        _ctx_expand_kernel,
        grid=(n // tm,),
        in_specs=[pl.BlockSpec((tm, KV_RANK), lambda i: (i, 0)), pl.BlockSpec((tm, HEAD_PAD), lambda i: (i, 0)),
                  _full(w["wku"].shape), _full(w["wvu"].shape), _full((1, HEAD_PAD))],
        out_specs=[pl.BlockSpec((tm, hp), lambda i: (i, 0)), pl.BlockSpec((tm, MLA_HEADS * V_DIM), lambda i: (i, 0))],
        out_shape=[jax.ShapeDtypeStruct((n, hp), BF16), jax.ShapeDtypeStruct((n, MLA_HEADS * V_DIM), BF16)],
        compiler_params=_params(1),
        name="ctx_expand",
    )(ckv, krp, w["wku"], w["wvu"], w["kg"])


def _softmax_pv(parts):
    m = functools.reduce(jnp.maximum, [jnp.max(s, axis=-1, keepdims=True) for s, _ in parts])
    acc, den = None, None
    for s, v in parts:
        e = jnp.exp(s - m)
        l = jnp.sum(e, axis=-1, keepdims=True)
        o = _mm(e, v)
        acc = o if acc is None else acc + o
        den = l if den is None else den + l
    return acc / den


def _ctx_attn_kernel(q_ref, k_ref, v_ref, nq_ref, nk_ref, nv_ref, o_ref):
    lo = _lane_iota((q_ref.shape[0], LANES)) < V_DIM
    for j in range(MLA_HEADS // 2):
        vp = v_ref[:, j * LANES:(j + 1) * LANES]
        outs = []
        for hd in (2 * j, 2 * j + 1):
            sl = slice(hd * HEAD_PAD, (hd + 1) * HEAD_PAD)
            s = _mm_nt(q_ref[:, sl], k_ref[:, sl])
            outs.append(_softmax_pv([(s, vp)]))
        o_ref[:, j * LANES:(j + 1) * LANES] = jnp.where(lo, outs[0], outs[1]).astype(o_ref.dtype)
    base = MLA_HEADS * V_DIM
    for j in range(NA_HEADS // 2):
        sl = slice(j * LANES, (j + 1) * LANES)
        qp = nq_ref[:, sl]
        kp = nk_ref[:, sl].astype(BF16)
        vp = nv_ref[:, sl].astype(BF16)
        outs = []
        for half in (0, 1):
            qm = jnp.where(lo if half == 0 else jnp.logical_not(lo), qp, jnp.zeros_like(qp))
            s = _mm_nt(qm, kp)
            outs.append(_softmax_pv([(s, vp)]))
        o_ref[:, base + j * LANES:base + (j + 1) * LANES] = jnp.where(lo, outs[0], outs[1]).astype(o_ref.dtype)


def _ctx_attn_call(lay, q, k, v, nq, nk, nv, out):
    s = lay.seq
    blk = lambda width: pl.BlockSpec((s, width), lambda b: (b, 0))
    return pl.pallas_call(
        _ctx_attn_body,
        grid=(lay.batch,),
        in_specs=[blk(q.shape[1]), blk(k.shape[1]), blk(v.shape[1]), blk(nq.shape[1]), blk(nk.shape[1]),
                  blk(nv.shape[1]), pl.BlockSpec(memory_space=pl.ANY)],
        out_specs=blk(out.shape[1]),
        out_shape=jax.ShapeDtypeStruct(out.shape, out.dtype),
        input_output_aliases={6: 0},
        compiler_params=_params(1),
        name="ctx_attn",
    )(q, k, v, nq, nk, nv, out)


def _ctx_attn_body(q_ref, k_ref, v_ref, nq_ref, nk_ref, nv_ref, alias_ref, o_ref):
    del alias_ref
    _ctx_attn_kernel(q_ref, k_ref, v_ref, nq_ref, nk_ref, nv_ref, o_ref)


def _mla_attn_kernel(q_ref, k_ref, v_ref, kc_ref, vc_ref, alias_ref, o_ref):
    del alias_ref
    lo = _lane_iota((q_ref.shape[0], LANES)) < V_DIM
    for j in range(MLA_HEADS // 2):
        vp = v_ref[:, j * LANES:(j + 1) * LANES]
        vcp = vc_ref[:, j * LANES:(j + 1) * LANES]
        outs = []
        for hd in (2 * j, 2 * j + 1):
            sl = slice(hd * HEAD_PAD, (hd + 1) * HEAD_PAD)
            qh = q_ref[:, sl]
            s1 = _mm_nt(qh, k_ref[:, sl])
            s2 = _mm_nt(qh, kc_ref[:, sl])
            outs.append(_softmax_pv([(s1, vp), (s2, vcp)]))
        o_ref[:, j * LANES:(j + 1) * LANES] = jnp.where(lo, outs[0], outs[1]).astype(o_ref.dtype)


def _mla_attn_call(lay, q, k, v, kc, vc, out):
    L, P = lay.dec_l, lay.past
    tq = min(256, L)
    qb = L // tq
    off_q = lay.n_ctx // tq
    off_l = lay.n_ctx // L
    hp = MLA_HEADS * HEAD_PAD
    vw = MLA_HEADS * V_DIM
    return pl.pallas_call(
        _mla_attn_kernel,
        grid=(lay.dec_b, qb),
        in_specs=[pl.BlockSpec((tq, hp), lambda b, i: (off_q + b * qb + i, 0)),
                  pl.BlockSpec((L, hp), lambda b, i: (off_l + b, 0)),
                  pl.BlockSpec((L, vw), lambda b, i: (off_l + b, 0)),
                  pl.BlockSpec((P, hp), lambda b, i: (b, 0)),
                  pl.BlockSpec((P, vw), lambda b, i: (b, 0)),
                  pl.BlockSpec(memory_space=pl.ANY)],
        out_specs=pl.BlockSpec((tq, vw), lambda b, i: (off_q + b * qb + i, 0)),
        out_shape=jax.ShapeDtypeStruct(out.shape, out.dtype),
        input_output_aliases={5: 0},
        compiler_params=_params(2),
        name="mla_attn",
    )(q, k, v, kc, vc, out)


def _na_bias_kernel(rpb_ref, o_ref):
    h = pl.program_id(0)
    n_dy, n_dx = 2 * NA_KH - 1, 2 * NA_KW - 1
    qc = lax.broadcasted_iota(jnp.int32, (GRID_W, LANES), 0)
    kc = _lane_iota((GRID_W, LANES)) & (GRID_W - 1)
    dx = jnp.clip(kc - qc, -(NA_KW - 1), NA_KW - 1) + NA_KW - 1
    cs = jnp.clip(qc - NA_KW // 2, 0, GRID_W - NA_KW)
    col_ok = (kc >= cs) & (kc < cs + NA_KW)
    tiles = []
    for dy in range(n_dy):
        acc = jnp.zeros((GRID_W, LANES), F32)
        for i in range(n_dx):
            acc = jnp.where(dx == i, rpb_ref[(h * n_dy + dy) * n_dx + i], acc)
        tiles.append(jnp.where(col_ok, acc, NEG_INF))
    zero = jnp.zeros((GRID_W, LANES), F32)
    tiles = [zero] + tiles + [zero]
    lo = _lane_iota((GRID_W, LANES)) < GRID_W
    for i in range(n_dy + 1):
        o_ref[i] = jnp.where(lo, tiles[i], tiles[i + 1])


def _na_bias_call(rpb):
    n_pairs = 2 * NA_KH
    return pl.pallas_call(
        _na_bias_kernel,
        grid=(NA_HEADS,),
        in_specs=[pl.BlockSpec(memory_space=pltpu.SMEM)],
        out_specs=pl.BlockSpec((n_pairs, GRID_W, LANES), lambda h: (h, 0, 0)),
        out_shape=jax.ShapeDtypeStruct((NA_HEADS * n_pairs, GRID_W, LANES), F32),
        compiler_params=_params(1),
        name="na_bias",
    )(rpb.reshape(-1))


def _na_attn_kernel(q_ref, k_ref, v_ref, kc_ref, vc_ref, tp_ref, alias_ref, o_ref, *, rows, win):
    del alias_ref
    g = pl.program_id(1)
    n_pairs = 2 * NA_KH
    nq = NA_QROWS * GRID_W
    nk = win * GRID_W
    ws = jnp.clip(NA_QROWS * g - NA_KH // 2, 0, rows - win)
    start = pl.multiple_of(ws * GRID_W, GRID_W)
    kwin = k_ref[pl.ds(start, nk), :].astype(BF16)
    vwin = v_ref[pl.ds(start, nk), :].astype(BF16)
    kctx = kc_ref[...].astype(BF16)
    vctx = vc_ref[...].astype(BF16)

    qr = NA_QROWS * g + (lax.broadcasted_iota(jnp.int32, (nq, nk), 0) >> GRID_SHIFT)
    kr = ws + (_lane_iota((nq, nk)) >> GRID_SHIFT)
    rs = jnp.clip(qr - NA_KH // 2, 0, rows - NA_KH)
    row_ok = (kr >= rs) & (kr < rs + NA_KH)

    lo = _lane_iota((nq, LANES)) < NA_HEAD_DIM
    for j in range(NA_HEADS // 2):
        sl = slice(j * LANES, (j + 1) * LANES)
        qp = q_ref[:, sl]
        kp, vp, kcp, vcp = kwin[:, sl], vwin[:, sl], kctx[:, sl], vctx[:, sl]
        outs = []
        for half in (0, 1):
            hd = 2 * j + half
            bias_rows = []
            for a in range(NA_QROWS):
                blocks = []
                for p in range(win // 2):
                    dy = ws + 2 * p - (NA_QROWS * g + a) + NA_KH - 1
                    idx = jnp.clip(dy + 1, 0, n_pairs - 1)
                    blocks.append(tp_ref[hd * n_pairs + idx])
                bias_rows.append(jnp.concatenate(blocks, axis=1))
            bias = jnp.concatenate(bias_rows, axis=0)
            qm = jnp.where(lo if half == 0 else jnp.logical_not(lo), qp, jnp.zeros_like(qp))
            s_w = jnp.where(row_ok, _mm_nt(qm, kp) + bias, NEG_INF)
            s_c = _mm_nt(qm, kcp)
            outs.append(_softmax_pv([(s_w, vp), (s_c, vcp)]))
        o_ref[:, sl] = jnp.where(lo, outs[0], outs[1]).astype(o_ref.dtype)


def _na_attn_call(lay, nq, nk, nv, kc, vc, tp, out):
    L, P = lay.dec_l, lay.past
    rows = L // GRID_W
    assert rows % NA_QROWS == 0
    win = min(2 * NA_KH, rows)
    groups = rows // NA_QROWS
    tq = NA_QROWS * GRID_W
    off_q = lay.n_ctx // tq
    off_l = lay.n_ctx // L
    n_out_blk = out.shape[1] // NA_WIDTH
    return pl.pallas_call(
        functools.partial(_na_attn_kernel, rows=rows, win=win),
        grid=(lay.dec_b, groups),
        in_specs=[pl.BlockSpec((tq, NA_WIDTH), lambda b, g: (off_q + b * groups + g, 0)),
                  pl.BlockSpec((L, NA_WIDTH), lambda b, g: (off_l + b, 0)),
                  pl.BlockSpec((L, NA_WIDTH), lambda b, g: (off_l + b, 0)),
                  pl.BlockSpec((P, NA_WIDTH), lambda b, g: (b, 0)),
                  pl.BlockSpec((P, NA_WIDTH), lambda b, g: (b, 0)),
                  _full(tp.shape),
                  pl.BlockSpec(memory_space=pl.ANY)],
        out_specs=pl.BlockSpec((tq, NA_WIDTH), lambda b, g: (off_q + b * groups + g, n_out_blk - 1)),
        out_shape=jax.ShapeDtypeStruct(out.shape, out.dtype),
        input_output_aliases={6: 0},
        compiler_params=_params(2),
        name="na_attn",
    )(nq, nk, nv, kc, vc, tp, out)


def _hy_filter_kernel(z_ref, w1_ref, b1_ref, fr_ref, w2_ref, b2_ref, w3a_ref, w3b_ref, w3c_ref, w3d_ref,
                      dl_ref, t_ref, alt_ref, wf_ref, c_ref, s_ref, kr_ref, ki_ref, kl_ref):
    h = jnp.sin(fr_ref[0:1, :] * (_mm_hi(z_ref[...], w1_ref[...]) + b1_ref[...]))
    h = jnp.sin(fr_ref[1:2, :] * (_mm_hi(h, w2_ref[...]) + b2_ref[...]))
    window = jnp.exp(-t_ref[...] * dl_ref[...]) + HY_MOD_SHIFT
    first = lax.broadcasted_iota(jnp.int32, window.shape, 0) == 0
    alt = alt_ref[...]
    wf = wf_ref[...]
    fwd = (w3a_ref, w3b_ref)
    bwd = (w3c_ref, w3d_ref)
    for o in range(HY_ORDER):
        hf = _mm_hi(h, fwd[o][...]) * window
        hb = jnp.where(first, 0.0, _mm_hi(h, bwd[o][...]) * window)
        norm = jnp.sum(jnp.abs(hf), axis=0, keepdims=True) + jnp.sum(jnp.abs(hb), axis=0, keepdims=True) + EPS
        a = (hf + hb) / norm
        d = (hf - hb) / norm
        kl_ref[o] = jnp.sum(a * alt, axis=0, keepdims=True) * (0.5 / a.shape[0])
        a, d = a.astype(BF16), d.astype(BF16)
        n_rows = a.shape[0]
        ch = min(DFT_ROWS, n_rows)
        for i in range(n_rows // ch):
            rows = slice(i * ch, (i + 1) * ch)
            kr_ref[o, rows, :] = jnp.dot(c_ref[rows, :], a, preferred_element_type=F32) * wf[rows]
            ki_ref[o, rows, :] = -jnp.dot(s_ref[rows, :], d, preferred_element_type=F32) * wf[rows]


def _hy_filter_call(L, z, p, consts):
    d = D_MODEL
    dc = 256
    nc = d // dc
    w3 = p["w3"]
    w3_spec = lambda k: pl.BlockSpec((HY_FF, dc), lambda c: (0, k * nc + c))
    zpad = z.shape[1]
    return pl.pallas_call(
        _hy_filter_kernel,
        grid=(nc,),
        in_specs=[_full((L, zpad)), _full((zpad, HY_FF)), _full((1, HY_FF)), _full((2, HY_FF)),
                  _full((HY_FF, HY_FF)), _full((1, HY_FF)), w3_spec(0), w3_spec(1), w3_spec(2), w3_spec(3),
                  pl.BlockSpec((1, dc), lambda c: (0, c)), _full((L, 1)), _full((L, 1)), _full((L, 1)),
                  _full((L, L)), _full((L, L))],
        out_specs=[pl.BlockSpec((HY_ORDER, L, dc), lambda c: (0, 0, c)),
                   pl.BlockSpec((HY_ORDER, L, dc), lambda c: (0, 0, c)),
                   pl.BlockSpec((HY_ORDER, 1, dc), lambda c: (0, 0, c))],
        out_shape=[jax.ShapeDtypeStruct((HY_ORDER, L, d), F32), jax.ShapeDtypeStruct((HY_ORDER, L, d), F32),
                   jax.ShapeDtypeStruct((HY_ORDER, 1, d), F32)],
        compiler_params=_params(1),
        name="hy_filter",
    )(z, p["w1"], p["b1"], p["freq"], p["w2"], p["b2"], w3, w3, w3, w3,
      consts["deltas"], consts["t"], consts["alt"], consts["wf"], consts["C"], consts["S"])


def _hy_stage_kernel(u_ref, g_ref, cwu_ref, cbu_ref, cwg_ref, cbg_ref, kr_ref, ki_ref, kl_ref, skip_ref, alt_ref,
                     c_ref, s_ref, *out_refs, conv_u):
    o_ref = out_refs[-1]
    L = u_ref.shape[0]
    row = lax.broadcasted_iota(jnp.int32, u_ref.shape, 0)
    first, last = row == 0, row == L - 1

    def short_conv(ref, w_ref, b_ref):
        x = ref[...]
        prev = jnp.where(first, 0.0, pltpu.roll(x, 1, 0))
        nxt = jnp.where(last, 0.0, pltpu.roll(x, L - 1, 0))
        w = w_ref[...]
        return prev * w[0:1] + x * w[1:2] + nxt * w[2:3] + b_ref[...]

    u = short_conv(u_ref, cwu_ref, cbu_ref) if conv_u else u_ref[...]
    gate = short_conv(g_ref, cwg_ref, cbg_ref)
    alt = alt_ref[...]
    ub = u.astype(BF16)
    ul_kl = jnp.sum(u * alt, axis=0, keepdims=True) * kl_ref[...]
    skip = skip_ref[...]
    ch = min(DFT_ROWS, L)
    yr, zi = [], []
    for i in range(L // ch):
        rows = slice(i * ch, (i + 1) * ch)
        ur = jnp.dot(c_ref[rows, :], ub, preferred_element_type=F32)
        us = jnp.dot(s_ref[rows, :], ub, preferred_element_type=F32)
        kr, ki = kr_ref[rows, :], ki_ref[rows, :]
        yr.append((ur * kr + us * ki).astype(BF16))
        zi.append((us * kr - ur * ki).astype(BF16))
    yr = jnp.concatenate(yr, axis=0)
    zi = jnp.concatenate(zi, axis=0)
    for i in range(L // ch):
        rows = slice(i * ch, (i + 1) * ch)
        y = jnp.dot(c_ref[rows, :], yr, preferred_element_type=F32) + jnp.dot(s_ref[rows, :], zi,
                                                                              preferred_element_type=F32)
        y = y + alt[rows] * ul_kl + u[rows] * skip
        o_ref[rows, :] = (gate[rows] * y).astype(o_ref.dtype)


def _hy_stage_call(u, u_blk, gate, g_blk, cwu, cbu, cwg, cbg, kr, ki, kl, skip, consts, *, conv_u, L, nseq,
                   row_off, out, td):
    d = D_MODEL
    nc = d // td
    off = row_off // L
    one = pl.Buffered(1)
    chan = lambda rows: pl.BlockSpec((rows, td), lambda c, b: (0, c))
    args = [u, gate, cwu, cbu, cwg, cbg, kr, ki, kl, skip, consts["alt"], consts["C"], consts["S"]]
    in_specs = [pl.BlockSpec((L, td), lambda c, b: (off + b, u_blk * nc + c)),
                pl.BlockSpec((L, td), lambda c, b: (off + b, g_blk * nc + c)),
                chan(3), chan(1), chan(3), chan(1),
                pl.BlockSpec((L, td), lambda c, b: (0, c), pipeline_mode=one),
                pl.BlockSpec((L, td), lambda c, b: (0, c), pipeline_mode=one),
                chan(1), chan(1),
                _full((L, 1)),
                pl.BlockSpec((L, L), lambda c, b: (0, 0), pipeline_mode=one),
                pl.BlockSpec((L, L), lambda c, b: (0, 0), pipeline_mode=one)]
    aliases = {}
    if not isinstance(out, jax.ShapeDtypeStruct):
        aliases = {len(args): 0}
        args.append(out)
        in_specs.append(pl.BlockSpec(memory_space=pl.ANY))
    return pl.pallas_call(
        functools.partial(_hy_stage_kernel, conv_u=conv_u),
        grid=(nc, nseq),
        in_specs=in_specs,
        out_specs=pl.BlockSpec((L, td), lambda c, b: (off + b, c)),
        out_shape=jax.ShapeDtypeStruct(out.shape, out.dtype),
        input_output_aliases=aliases,
        compiler_params=_params(2),
        name="hy_stage_%d" % L,
    )(*args)


def _dft_consts(L):
    f = jnp.arange(L, dtype=jnp.int32)
    ang = ((f[:, None] * f[None, :]) % (2 * L)).astype(F32) * (math.pi / L)
    t = jnp.linspace(0.0, 1.0, L, dtype=F32)[:, None]
    w = 2.0 * math.pi * jnp.arange(L, dtype=F32)[:, None] / L
    bands = jnp.linspace(1e-4, HY_BANDS - 1, HY_BANDS, dtype=F32)[None, :]
    z = jnp.concatenate([t, jnp.cos(bands * w), -jnp.sin(bands * w)], axis=-1)
    z = jnp.pad(z, ((0, 0), (0, 32 - HY_EMB)))
    deltas = jnp.abs(jnp.linspace(math.log(HY_TARGET) / HY_FAST_DECAY, math.log(HY_TARGET) / HY_SLOW_DECAY,
                                  D_MODEL, dtype=F32))[None, :]
    alt = (1.0 - 2.0 * (f % 2).astype(F32))[:, None]
    wf = jnp.where(f == 0, 0.5 / L, 1.0 / L).astype(F32)[:, None]
    return {"C": jnp.cos(ang).astype(BF16), "S": jnp.sin(ang).astype(BF16), "t": t, "z": z, "deltas": deltas,
            "alt": alt, "wf": wf}


def _pack_bf16_pairs(x):
    half = x.shape[1] // 2
    bits = lax.bitcast_convert_type(x.astype(BF16).astype(F32), jnp.uint32)
    return (bits[:, half:] & jnp.uint32(0xFFFF0000)) | (bits[:, :half] >> 16)


def _unpack_bf16_pairs(p):
    lo = lax.bitcast_convert_type(p << 16, F32)
    hi = lax.bitcast_convert_type(p & jnp.uint32(0xFFFF0000), F32)
    return jnp.concatenate([lo, hi], axis=1).astype(BF16)


def _route_kernel(x_ref, g_ref, sc_ref, sh_ref, wr_ref, br_ref, h_ref, meta_ref, cnt_ref, base_ref):
    i = pl.program_id(0)
    tm = x_ref.shape[0]

    @pl.when(i == 0)
    def _():
        base_ref[...] = jnp.zeros_like(base_ref)

    h = _norm_mod(x_ref[...], g_ref[...], sc_ref[...], sh_ref[...])
    h_ref[...] = _pack_bf16_pairs(h)
    logits = _mm_hi(h, wr_ref[...]) + br_ref[...]
    lane_i = _lane_iota(logits.shape)
    lane = lane_i.astype(F32)
    grp_of_lane = (lane_i >> 3).astype(F32)
    big = 1e6
    is_g = (lane_i >= N_EXPERTS) & (lane_i < N_EXPERTS + N_GROUPS)
    glog = jnp.where(is_g, logits, NEG_INF)
    gmax = jnp.max(glog, axis=-1, keepdims=True)
    g_w = 1.0 / jnp.sum(jnp.exp(glog - gmax), axis=-1, keepdims=True)
    g_i = jnp.min(jnp.where(glog == gmax, lane - N_EXPERTS, big), axis=-1, keepdims=True)
    in_grp = (lane_i < N_EXPERTS) & (grp_of_lane == g_i)
    elog = jnp.where(in_grp, logits, NEG_INF)
    m1 = jnp.max(elog, axis=-1, keepdims=True)
    i1 = jnp.min(jnp.where(elog == m1, lane, big), axis=-1, keepdims=True)
    elog2 = jnp.where(lane == i1, NEG_INF, elog)
    m2 = jnp.max(elog2, axis=-1, keepdims=True)
    i2 = jnp.min(jnp.where(elog2 == m2, lane, big), axis=-1, keepdims=True)
    a2 = jnp.exp(m2 - m1)
    w1 = g_w / (1.0 + a2)
    w2 = g_w * a2 / (1.0 + a2)

    oh1 = jnp.where(lane == i1, 1.0, 0.0).astype(BF16)
    oh2 = jnp.where(lane == i2, 1.0, 0.0).astype(BF16)
    tri = jnp.where(lax.broadcasted_iota(jnp.int32, (tm, tm), 0) > lax.broadcasted_iota(jnp.int32, (tm, tm), 1),
                    1.0, 0.0).astype(BF16)
    cum1 = jnp.dot(tri, oh1, preferred_element_type=F32)
    cum2 = jnp.dot(tri, oh2, preferred_element_type=F32)
    tot1 = jnp.sum(oh1.astype(F32), axis=0, keepdims=True)
    tot2 = jnp.sum(oh2.astype(F32), axis=0, keepdims=True)
    base = base_ref[...]
    r1 = jnp.sum(jnp.where(lane == i1, base + cum1, 0.0), axis=-1, keepdims=True)
    r2 = jnp.sum(jnp.where(lane == i2, base + tot1 + cum2, 0.0), axis=-1, keepdims=True)
    base = base + tot1 + tot2
    base_ref[...] = base
    cnt_ref[...] = base
    cols = [i1, i2, r1, r2, w1, w2]
    meta = jnp.zeros(logits.shape, F32)
    for c, val in enumerate(cols):
        meta = jnp.where(lane_i == c, val, meta)
    meta_ref[...] = meta


def _route_call(lay, x, norm, wr, br):
    n, d = x.shape
    tm = lay.tm
    g, sc, sh = norm
    tok = lambda width: pl.BlockSpec((tm, width), lambda i: (i, 0))
    return pl.pallas_call(
        _route_kernel,
        grid=(n // tm,),
        in_specs=[tok(d), _full((1, d)), lay.mod_spec(), lay.mod_spec(), _full((d, LANES)), _full((1, LANES))],
        out_specs=[tok(d // 2), tok(LANES), _full((1, LANES))],
        out_shape=[jax.ShapeDtypeStruct((n, d // 2), jnp.uint32), jax.ShapeDtypeStruct((n, LANES), F32),
                   jax.ShapeDtypeStruct((1, LANES), F32)],
        scratch_shapes=[pltpu.VMEM((1, LANES), F32)],
        compiler_params=_params(1),
        name="moe_route",
    )(x, g, sc, sh, wr, br)


def _row_copy(src_ref, s, dst_ref, d, sem):
    return pltpu.make_async_copy(src_ref.at[pl.ds(s, 1)], dst_ref.at[pl.ds(d, 1)], sem)


def _expert_kernel(te_ref, src_ref, dst_ref, h_ref, wg_ref, wu_ref, wd_ref, o_ref,
                   xbuf, ybuf, zbuf, sem_out, sem_z, *, trash_base):
    j = pl.program_id(0)
    tiles = pl.num_programs(0)
    used = te_ref[tiles]
    slot = lax.rem(j, 2)
    tm = xbuf.shape[0]

    def start_scatter(s):
        for r in range(tm):
            _row_copy(ybuf.at[s], r, o_ref, dst_ref[0, r], sem_out.at[s]).start(priority=r % 2)

    def wait_scatter(s):
        for r in range(tm):
            _row_copy(ybuf.at[s], r, o_ref, 0, sem_out.at[s]).wait()

    @pl.when(j == 0)
    def _():
        zbuf[...] = jnp.zeros_like(zbuf)
        fills = [pltpu.make_async_copy(zbuf, o_ref.at[pl.ds(trash_base + k * tm, tm)], sem_z)
                 for k in range(N_EXPERTS)]
        for f in fills:
            f.start()
        for f in fills:
            f.wait()

    @pl.when(j < used)
    def _():
        for r in range(tm):
            xbuf[r:r + 1, :] = h_ref[pl.ds(src_ref[0, r], 1), :]

        @pl.when(j >= 2)
        def _():
            wait_scatter(slot)

        x = _unpack_bf16_pairs(xbuf[...])
        hg = _mm(x, wg_ref[...])
        hu = _mm(x, wu_ref[...])
        act = hg * jax.nn.sigmoid(hg) * hu
        ybuf[slot] = _mm(act, wd_ref[...])
        start_scatter(slot)

        @pl.when(j == used - 1)
        def _():
            wait_scatter(slot)

            @pl.when(j >= 1)
            def _():
                wait_scatter(1 - slot)


def _expert_call(layer, tile_expert, src, dst, h, wg, wu, wd):
    n, half = h.shape
    d = 2 * half
    f = wg.shape[-1]
    tiles = src.shape[0]
    tm = MOE_TILE
    idx = pl.BlockSpec((None, 1, tm), lambda j, te: (j, 0, 0), memory_space=pltpu.SMEM)
    wspec = lambda a, b: pl.BlockSpec((None, None, a, b), lambda j, te: (layer, te[j], 0, 0))
    grid_spec = pltpu.PrefetchScalarGridSpec(
        num_scalar_prefetch=1,
        grid=(tiles,),
        in_specs=[idx, idx,
                  pl.BlockSpec((n, half), lambda j, te: (0, 0), pipeline_mode=pl.Buffered(1)),
                  wspec(d, f), wspec(d, f), wspec(f, d)],
        out_specs=pl.BlockSpec(memory_space=pl.ANY),
        scratch_shapes=[pltpu.VMEM((tm, half), jnp.uint32), pltpu.VMEM((2, tm, d), F32), pltpu.VMEM((tm, d), F32),
                        pltpu.SemaphoreType.DMA((2,)), pltpu.SemaphoreType.DMA(())],
    )
    return pl.pallas_call(
        functools.partial(_expert_kernel, trash_base=2 * n),
        grid_spec=grid_spec,
        out_shape=jax.ShapeDtypeStruct((2 * n + N_EXPERTS * tm, d), F32),
        compiler_params=_params(1),
        name="moe_experts",
    )(tile_expert, src, dst, h, wg, wu, wd)


def _plan_kernel(pos_ref, inv_ref):
    i = pl.program_id(0)
    ch = pos_ref.shape[1]

    @pl.when(i == 0)
    def _():
        def clear(p, c):
            inv_ref[p] = -1
            return c

        lax.fori_loop(0, inv_ref.shape[0], clear, 0, unroll=8)

    def place(t, c):
        inv_ref[pos_ref[0, t]] = i * ch + t
        return c

    lax.fori_loop(0, ch, place, 0, unroll=8)


def _plan_call(pos, rows):
    total = pos.shape[0]
    ch = min(4096, total)
    assert total % ch == 0
    return pl.pallas_call(
        _plan_kernel,
        grid=(total // ch,),
        in_specs=[pl.BlockSpec((None, 1, ch), lambda i: (i, 0, 0), memory_space=pltpu.SMEM)],
        out_specs=pl.BlockSpec(memory_space=pltpu.SMEM),
        out_shape=jax.ShapeDtypeStruct((rows,), jnp.int32),
        compiler_params=_params(1),
        name="moe_plan",
    )(pos.reshape(total // ch, 1, ch))


def _combine_kernel(x_ref, gate_ref, meta_ref, y0_ref, y1_ref, o_ref):
    meta = meta_ref[...]
    o_ref[...] = x_ref[...] + gate_ref[...] * (meta[:, 4:5] * y0_ref[...] + meta[:, 5:6] * y1_ref[...])


def _combine_call(lay, x, gate, meta, ys):
    n, d = x.shape
    tm = lay.tm
    tok = lambda width: pl.BlockSpec((tm, width), lambda i: (i, 0))
    return pl.pallas_call(
        _combine_kernel,
        grid=(n // tm,),
        in_specs=[tok(d), lay.mod_spec(), tok(LANES), tok(d), pl.BlockSpec((tm, d), lambda i: (n // tm + i, 0))],
        out_specs=tok(d),
        out_shape=jax.ShapeDtypeStruct((n, d), F32),
        compiler_params=_params(1),
        name="moe_combine",
    )(x, gate, meta, ys, ys)


def _moe_layer(lay, layer, x, norm, gate, wr, br, wg, wu, wd):
    n = x.shape[0]
    tm = MOE_TILE
    h, meta, counts = _route_call(lay, x, norm, wr, br)
    cnt = counts[0, :N_EXPERTS].astype(jnp.int32)
    padded = ((cnt + tm - 1) // tm) * tm
    ends = jnp.cumsum(padded)
    starts = ends - padded
    m4 = meta[:, 0:4].T.astype(jnp.int32)
    onehot = m4[0:2, :, None] == jnp.arange(N_EXPERTS, dtype=jnp.int32)[None, None, :]
    pos = jnp.sum(jnp.where(onehot, starts[None, None, :], 0), axis=-1) + m4[2:4]
    rows = 2 * n + N_EXPERTS * tm
    tiles = rows // tm
    inv = _plan_call(pos.reshape(-1), rows)
    is_pad = inv < 0
    trash = 2 * n - 1 + jnp.cumsum(is_pad.astype(jnp.int32))
    dst = jnp.where(is_pad, trash, inv).reshape(tiles, 1, tm)
    src = jnp.where(is_pad, 0, jnp.where(inv >= n, inv - n, inv)).reshape(tiles, 1, tm)
    tile_start = jnp.arange(tiles, dtype=jnp.int32) * tm
    tile_expert = jnp.sum((tile_start[:, None] >= ends[None, :]).astype(jnp.int32), axis=1)
    used = ends[-1] // tm
    last = jnp.take(tile_expert, jnp.maximum(used - 1, 0))
    tile_expert = jnp.where(tile_start // tm < used, tile_expert, last)
    tile_expert = jnp.concatenate([tile_expert, used[None]]).astype(jnp.int32)
    ys = _expert_call(layer, tile_expert, src, dst, h, wg, wu, wd)
    return _combine_call(lay, x, gate, meta, ys)


def _attn_weights(a, attn_w_in, mla_q_norm, mla_kv_norm, mla_w_q_up, mla_w_kv_up, mla_q_gain, mla_k_gain,
                  na_q_gain, na_k_gain):
    w_in = attn_w_in[a]
    o1, o2, o3 = Q_RANK, Q_RANK + KV_RANK, Q_RANK + KV_RANK + ROPE_DIM
    pad_head = lambda g: jnp.pad(g, (0, HEAD_PAD - QK_DIM))[None, :]
    wkr = jnp.zeros((D_MODEL, HEAD_PAD), F32).at[:, NOPE_DIM:QK_DIM].set(w_in[:, o2:o3])
    wqu = jnp.pad(mla_w_q_up[a].reshape(Q_RANK, MLA_HEADS, QK_DIM), ((0, 0), (0, 0), (0, HEAD_PAD - QK_DIM)))
    wkv = mla_w_kv_up[a].reshape(KV_RANK, MLA_HEADS, NOPE_DIM + V_DIM)
    wku = jnp.pad(wkv[:, :, :NOPE_DIM], ((0, 0), (0, 0), (0, HEAD_PAD - NOPE_DIM)))
    return {
        "wql": w_in[:, :o1].astype(BF16), "wkvl": w_in[:, o1:o2].astype(BF16), "wkr": wkr.astype(BF16),
        "wna": w_in[:, o3:].astype(BF16),
        "qn": mla_q_norm[a][None, :], "kvn": mla_kv_norm[a][None, :],
        "wqu": wqu.reshape(Q_RANK, MLA_HEADS * HEAD_PAD).astype(BF16),
        "wku": wku.reshape(KV_RANK, MLA_HEADS * HEAD_PAD).astype(BF16),
        "wvu": wkv[:, :, NOPE_DIM:].reshape(KV_RANK, MLA_HEADS * V_DIM).astype(BF16),
        "qg": pad_head(mla_q_gain[a]) * MLA_SCALE, "kg": pad_head(mla_k_gain[a]),
        "nqg": jnp.tile(na_q_gain[a], LANES // NA_HEAD_DIM)[None, :] * NA_SCALE,
        "nkg": jnp.tile(na_k_gain[a], LANES // NA_HEAD_DIM)[None, :],
    }


def _rope_tables(L):
    t = jnp.arange(L)
    quarter = ROPE_DIM // 4
    inv = ROPE_THETA ** (-jnp.arange(quarter, dtype=F32) / quarter)
    ang_r = (t // GRID_W).astype(F32)[:, None] * inv[None, :]
    ang_c = (t % GRID_W).astype(F32)[:, None] * inv[None, :]
    zeros = jnp.zeros((L, quarter), F32)
    ones_nope = jnp.ones((L, NOPE_DIM), F32)
    pad = jnp.zeros((L, HEAD_PAD - QK_DIM), F32)
    cos = jnp.concatenate([ones_nope, jnp.cos(ang_r), jnp.cos(ang_r), jnp.cos(ang_c), jnp.cos(ang_c), pad], axis=1)
    s_lo = jnp.concatenate([0 * ones_nope, -jnp.sin(ang_r), zeros, -jnp.sin(ang_c), zeros, pad], axis=1)
    s_hi = jnp.concatenate([0 * ones_nope, zeros, jnp.sin(ang_r), zeros, jnp.sin(ang_c), pad], axis=1)
    return cos, s_lo, s_hi


def kernel(x_prompt, x_sample, cache_mla_ckv, cache_mla_krope, cache_na_k, cache_na_v, c, c_ctx, ada_w, ada_b, norm_mix, norm_ffn, attn_w_in, mla_q_norm, mla_kv_norm, mla_w_q_up, mla_w_kv_up, mla_q_gain, mla_k_gain, na_q_gain, na_k_gain, na_rpb, attn_w_out, hy_w_in, hy_b_in, hy_conv_w, hy_conv_b, hy_filt_w1, hy_filt_b1, hy_filt_freq, hy_filt_w2, hy_filt_b2, hy_filt_w3, hy_skip, hy_w_out, moe_w_group, moe_b_group, moe_w_router, moe_b_router, moe_w_gate, moe_w_up, moe_w_down):
    batch, seq, d = x_prompt.shape
    dec_b, dec_l, _ = x_sample.shape
    past = cache_mla_ckv.shape[2]
    depth = ada_w.shape[0]
    lay = _Layout(batch, seq, dec_b, dec_l, past)
    n_ctx = lay.n_ctx

    x = jnp.concatenate([x_prompt.reshape(n_ctx, d), x_sample.reshape(lay.n_dec, d)], axis=0)
    cvec = jnp.concatenate([c_ctx[None, :], c], axis=0)
    cvec_t = jnp.pad(cvec, ((0, MOD_ROWS - cvec.shape[0]), (0, 0))).T
    mods = _ada_call(cvec_t, ada_w, ada_b, 1 + dec_b)

    def mod(l, j):
        return mods[l, :, j * d:(j + 1) * d].reshape(MOD_ROWS, 1, d)

    rope = _rope_tables(dec_l)
    consts = {L: _dft_consts(L) for L in sorted({seq, dec_l})}
    ckv_out, krope_out, nk_out, nv_out = [], [], [], []

    for l in range(depth):
        sh1, sc1, g1, sh2, sc2, g2 = [mod(l, j) for j in range(6)]
        norm1 = (norm_mix[l][None, :], sc1, sh1)
        if l % 2 == 0:
            a = l // 2
            w = _attn_weights(a, attn_w_in, mla_q_norm, mla_kv_norm, mla_w_q_up, mla_w_kv_up, mla_q_gain,
                              mla_k_gain, na_q_gain, na_k_gain)
            q, k, v, nq, nk, nv, ckv, krp = _attn_in_call(lay, x, norm1, w, rope)
            ckv_out.append(ckv[:n_ctx].reshape(batch, seq, KV_RANK))
            krope_out.append(krp[:n_ctx, NOPE_DIM:QK_DIM].reshape(batch, seq, ROPE_DIM))
            nk_out.append(nk[:n_ctx].reshape(batch, seq, NA_HEADS, NA_HEAD_DIM))
            nv_out.append(nv[:n_ctx].reshape(batch, seq, NA_HEADS, NA_HEAD_DIM))
            krp_c = jnp.pad(cache_mla_krope[:, a].reshape(dec_b * past, ROPE_DIM),
                            ((0, 0), (NOPE_DIM, HEAD_PAD - QK_DIM)))
            kc, vc = _ctx_expand_call(cache_mla_ckv[:, a].reshape(dec_b * past, KV_RANK), krp_c, w)
            tp = _na_bias_call(na_rpb[a])
            att = jnp.zeros((lay.n, MLA_HEADS * V_DIM + NA_WIDTH), BF16)
            att = _ctx_attn_call(lay, q, k, v, nq, nk, nv, att)
            att = _mla_attn_call(lay, q, k, v, kc, vc, att)
            att = _na_attn_call(lay, nq, nk, nv, cache_na_k[:, a].reshape(dec_b * past, NA_WIDTH),
                                cache_na_v[:, a].reshape(dec_b * past, NA_WIDTH), tp, att)
            x = _dense_call(lay, att, attn_w_out[a].astype(BF16), resid=(x, g1), name="attn_out")
        else:
            j = l // 2
            p3 = _dense_call(lay, x, hy_w_in[j].astype(BF16), norm=norm1, bias=hy_b_in[j][None, :], name="hy_in")
            cw = hy_conv_w[j].reshape(3, 3, d).transpose(1, 0, 2)
            cb = hy_conv_b[j].reshape(3, 1, d)
            skip = hy_skip[j].reshape(HY_ORDER, 1, d)
            filt = {"w1": jnp.pad(hy_filt_w1[j], ((0, 32 - HY_EMB), (0, 0))), "b1": hy_filt_b1[j][None, :],
                    "freq": hy_filt_freq[j], "w2": hy_filt_w2[j], "b2": hy_filt_b2[j][None, :],
                    "w3": hy_filt_w3[j]}
            z1 = jnp.zeros((lay.n, d), F32)
            z = jnp.zeros((lay.n, d), BF16)
            for L, nseq, off, td in ((seq, batch, 0, d), (dec_l, dec_b, n_ctx, 256)):
                kr, ki, kl = _hy_filter_call(L, consts[L]["z"], filt, consts[L])
                geom = dict(L=L, nseq=nseq, row_off=off, td=min(td, d))
                z1 = _hy_stage_call(p3, 0, p3, 1, cw[0], cb[0], cw[1], cb[1], kr[0], ki[0], kl[0], skip[0],
                                    consts[L], conv_u=True, out=z1, **geom)
                z = _hy_stage_call(z1, 0, p3, 2, cw[0], cb[0], cw[2], cb[2], kr[1], ki[1], kl[1], skip[1],
                                   consts[L], conv_u=False, out=z, **geom)
            x = _dense_call(lay, z, hy_w_out[j].astype(BF16), resid=(x, g1), name="hy_out")
        wr = jnp.zeros((d, LANES), F32).at[:, :N_EXPERTS].set(moe_w_router[l])
        wr = wr.at[:, N_EXPERTS:N_EXPERTS + N_GROUPS].set(moe_w_group[l])
        br = jnp.zeros((1, LANES), F32).at[0, :N_EXPERTS].set(moe_b_router[l])
        br = br.at[0, N_EXPERTS:N_EXPERTS + N_GROUPS].set(moe_b_group[l])
        x = _moe_layer(lay, l, x, (norm_ffn[l][None, :], sc2, sh2), g2, wr, br, moe_w_gate, moe_w_up, moe_w_down)

    y_prompt = x[:n_ctx].reshape(batch, seq, d)
    y_sample = x[n_ctx:].reshape(dec_b, dec_l, d)
    return (y_prompt, y_sample, jnp.stack(ckv_out, axis=1), jnp.stack(krope_out, axis=1),
            jnp.stack(nk_out, axis=1), jnp.stack(nv_out, axis=1))
```

```python
import functools
import math

import jax
import jax.numpy as jnp
import numpy as np
from jax import lax
from jax.experimental import pallas as pl
from jax.experimental.pallas import tpu as pltpu

F32 = jnp.float32
BF16 = jnp.bfloat16
HIGHEST = lax.Precision.HIGHEST

D_MODEL = 1024
DEPTH = 4
GRID_W = 64
GRID_SHIFT = 6
MLA_HEADS = 8
Q_RANK = 384
KV_RANK = 128
NOPE_DIM = 64
ROPE_DIM = 32
V_DIM = 64
QK_DIM = NOPE_DIM + ROPE_DIM
ROPE_THETA = 10000.0
NA_HEADS = 8
NA_HEAD_DIM = 64
NA_KH = 8
NA_KW = 16
NA_WIDTH = NA_HEADS * NA_HEAD_DIM
HY_ORDER = 2
HY_BANDS = 8
HY_EMB = 1 + 2 * HY_BANDS
HY_FF = 64
HY_TARGET = 1e-2
HY_FAST_DECAY = 0.3
HY_SLOW_DECAY = 1.5
HY_MOD_SHIFT = 0.05
N_GROUPS = 4
EXPERTS_PER_GROUP = 8
N_EXPERTS = N_GROUPS * EXPERTS_PER_GROUP
D_EXPERT = 256
EPS = 1e-6
NEG_INF = -1e30

LANES = 128
HEAD_PAD = LANES
MOD_ROWS = 8
VMEM_LIMIT = 56 * 1024 * 1024

MLA_SCALE = QK_DIM ** -0.5
NA_SCALE = NA_HEAD_DIM ** -0.5

NA_QROWS = 8
MOE_TILE = 256
TOK_TILE = 256
DFT_ROWS = 512


def _params(n_axes, vmem=VMEM_LIMIT):
    return pltpu.CompilerParams(dimension_semantics=("arbitrary",) * n_axes, vmem_limit_bytes=vmem)


def _mm(a, b):
    return jnp.dot(a.astype(BF16), b.astype(BF16), preferred_element_type=F32)


def _mm_nt(a, b):
    return lax.dot_general(a.astype(BF16), b.astype(BF16), (((1,), (1,)), ((), ())),
                           preferred_element_type=F32)


def _mm_hi(a, b):
    return jnp.dot(a, b, preferred_element_type=F32, precision=HIGHEST)


def _norm_mod(x, g, sc, sh):
    ms = jnp.mean(x * x, axis=-1, keepdims=True)
    return (x * lax.rsqrt(ms + EPS) * g) * (1.0 + sc) + sh


def _lane_iota(shape):
    return lax.broadcasted_iota(jnp.int32, shape, len(shape) - 1)


class _Layout:
    def __init__(self, batch, seq, dec_b, dec_l, past):
        self.batch, self.seq, self.dec_b, self.dec_l, self.past = batch, seq, dec_b, dec_l, past
        self.n_ctx = batch * seq
        self.n_dec = dec_b * dec_l
        self.n = self.n_ctx + self.n_dec
        self.tm = min(TOK_TILE, seq, dec_l)
        assert self.n_ctx % self.tm == 0 and dec_l % self.tm == 0
        assert self.n_ctx % dec_l == 0 and dec_l % GRID_W == 0
        assert 1 + dec_b <= MOD_ROWS

    def seg(self, i):
        a = self.n_ctx // self.tm
        b = self.dec_l // self.tm
        return jnp.where(i < a, 0, 1 + (i - a) // b)

    def mod_spec(self):
        return pl.BlockSpec((None, 1, D_MODEL), lambda i: (self.seg(i), 0, 0))


def _full(shape):
    nd = len(shape)
    return pl.BlockSpec(shape, lambda *_: (0,) * nd)


def _ada_kernel(ct_ref, w_ref, b_ref, o_ref, *, rows):
    c = ct_ref[...]
    s = c * jax.nn.sigmoid(c)
    w = w_ref[...]
    out = [jnp.sum(s[:, r:r + 1] * w, axis=0, keepdims=True) for r in range(rows)]
    out += [jnp.zeros_like(out[0])] * (MOD_ROWS - rows)
    o_ref[...] = jnp.concatenate(out, axis=0) + b_ref[...]


def _ada_call(cvec_t, ada_w, ada_b, rows):
    depth, d, n6 = ada_w.shape
    tn = 1536
    return pl.pallas_call(
        functools.partial(_ada_kernel, rows=rows),
        grid=(depth, n6 // tn),
        in_specs=[
            pl.BlockSpec((d, MOD_ROWS), lambda l, j: (0, 0)),
            pl.BlockSpec((None, d, tn), lambda l, j: (l, 0, j)),
            pl.BlockSpec((None, 1, tn), lambda l, j: (l, 0, j)),
        ],
        out_specs=pl.BlockSpec((None, MOD_ROWS, tn), lambda l, j: (l, 0, j)),
        out_shape=jax.ShapeDtypeStruct((depth, MOD_ROWS, n6), F32),
        compiler_params=_params(2),
        name="ada_mod",
    )(cvec_t, ada_w, ada_b.reshape(depth, 1, n6))


def _dense_kernel(*refs, has_norm, has_bias, has_resid):
    it = iter(refs)
    x_ref = next(it)
    if has_norm:
        g_ref, sc_ref, sh_ref = next(it), next(it), next(it)
    w_ref = next(it)
    b_ref = next(it) if has_bias else None
    if has_resid:
        r_ref, gate_ref = next(it), next(it)
    o_ref = next(it)
    x = x_ref[...]
    if has_norm:
        x = _norm_mod(x.astype(F32), g_ref[...], sc_ref[...], sh_ref[...])
    y = _mm(x, w_ref[...])
    if has_bias:
        y = y + b_ref[...]
    if has_resid:
        y = r_ref[...] + gate_ref[...] * y
    o_ref[...] = y.astype(o_ref.dtype)


def _dense_call(lay, x, w, *, norm=None, bias=None, resid=None, out_dtype=F32, name):
    n, k = x.shape
    m = w.shape[1]
    tm = lay.tm
    args, specs = [x], [pl.BlockSpec((tm, k), lambda i: (i, 0))]
    if norm is not None:
        g, sc, sh = norm
        args += [g, sc, sh]
        specs += [_full((1, k)), lay.mod_spec(), lay.mod_spec()]
    args.append(w)
    specs.append(_full((k, m)))
    if bias is not None:
        args.append(bias)
        specs.append(_full((1, m)))
    if resid is not None:
        r, gate = resid
        args += [r, gate]
        specs += [pl.BlockSpec((tm, m), lambda i: (i, 0)), lay.mod_spec()]
    return pl.pallas_call(
        functools.partial(_dense_kernel, has_norm=norm is not None, has_bias=bias is not None,
                          has_resid=resid is not None),
        grid=(n // tm,),
        in_specs=specs,
        out_specs=pl.BlockSpec((tm, m), lambda i: (i, 0)),
        out_shape=jax.ShapeDtypeStruct((n, m), out_dtype),
        compiler_params=_params(1),
        name=name,
    )(*args)


def _head_rms(xh, gain):
    ss = jnp.sum(xh * xh, axis=-1, keepdims=True) * (1.0 / QK_DIM)
    return xh * lax.rsqrt(ss + EPS) * gain


def _rope(xh, c, s_lo, s_hi):
    quarter = ROPE_DIM // 4
    return xh * c + pltpu.roll(xh, HEAD_PAD - quarter, 1) * s_lo + pltpu.roll(xh, quarter, 1) * s_hi


def _pair_rms(x, gain):
    lo = _lane_iota(x.shape) < NA_HEAD_DIM
    x2 = x * x
    s_lo = jnp.sum(jnp.where(lo, x2, 0.0), axis=-1, keepdims=True) * (1.0 / NA_HEAD_DIM)
    s_hi = jnp.sum(jnp.where(lo, 0.0, x2), axis=-1, keepdims=True) * (1.0 / NA_HEAD_DIM)
    r = jnp.where(lo, lax.rsqrt(s_lo + EPS), lax.rsqrt(s_hi + EPS))
    return x * r * gain


def _attn_in_kernel(x_ref, g_ref, sc_ref, sh_ref, wql_ref, wkvl_ref, wkr_ref, wna_ref, qn_ref, kvn_ref,
                    wqu_ref, wku_ref, wvu_ref, qg_ref, kg_ref, nqg_ref, nkg_ref, rc_ref, rlo_ref, rhi_ref,
                    q_ref, k_ref, v_ref, nq_ref, nk_ref, nv_ref, ckv_ref, krp_ref, *, ctx_tiles):
    latent = (pl.program_id(0) >= ctx_tiles).astype(F32)
    rc = 1.0 + latent * (rc_ref[...] - 1.0)
    rlo = latent * rlo_ref[...]
    rhi = latent * rhi_ref[...]

    h = _norm_mod(x_ref[...], g_ref[...], sc_ref[...], sh_ref[...]).astype(BF16)
    q_lat = _mm(h, wql_ref[...])
    kv_lat = _mm(h, wkvl_ref[...])
    krp = _mm(h, wkr_ref[...])
    na = _mm(h, wna_ref[...])

    def rms(x, g):
        return x * lax.rsqrt(jnp.mean(x * x, axis=-1, keepdims=True) + EPS) * g

    ckv = rms(kv_lat, kvn_ref[...])
    ckv_ref[...] = ckv
    krp_ref[...] = krp
    q_up = _mm(rms(q_lat, qn_ref[...]), wqu_ref[...])
    k_up = _mm(ckv, wku_ref[...])
    v_ref[...] = _mm(ckv, wvu_ref[...]).astype(v_ref.dtype)
    qg, kg = qg_ref[...], kg_ref[...]
    for hd in range(MLA_HEADS):
        sl = slice(hd * HEAD_PAD, (hd + 1) * HEAD_PAD)
        q_ref[:, sl] = _rope(_head_rms(q_up[:, sl], qg), rc, rlo, rhi).astype(q_ref.dtype)
        k_ref[:, sl] = _rope(_head_rms(k_up[:, sl] + krp, kg), rc, rlo, rhi).astype(k_ref.dtype)
    nqg, nkg = nqg_ref[...], nkg_ref[...]
    for j in range(NA_WIDTH // LANES):
        sl = slice(j * LANES, (j + 1) * LANES)
        nq_ref[:, sl] = _pair_rms(na[:, j * LANES:(j + 1) * LANES], nqg).astype(nq_ref.dtype)
        nk_ref[:, sl] = _pair_rms(na[:, NA_WIDTH + j * LANES:NA_WIDTH + (j + 1) * LANES], nkg)
    nv_ref[...] = na[:, 2 * NA_WIDTH:]


def _attn_in_call(lay, x, norm, w, rope):
    n, d = x.shape
    tm = lay.tm
    ctx_tiles = lay.n_ctx // tm
    dec_tiles = lay.dec_l // tm
    g, sc, sh = norm

    def rope_map(i):
        return (jnp.where(i < ctx_tiles, 0, (i - ctx_tiles) % dec_tiles), 0)

    tok = lambda width: pl.BlockSpec((tm, width), lambda i: (i, 0))
    hp = MLA_HEADS * HEAD_PAD
    in_specs = [tok(d), _full((1, d)), lay.mod_spec(), lay.mod_spec(),
                _full(w["wql"].shape), _full(w["wkvl"].shape), _full(w["wkr"].shape), _full(w["wna"].shape),
                _full((1, Q_RANK)), _full((1, KV_RANK)),
                _full(w["wqu"].shape), _full(w["wku"].shape), _full(w["wvu"].shape),
                _full((1, HEAD_PAD)), _full((1, HEAD_PAD)), _full((1, LANES)), _full((1, LANES)),
                pl.BlockSpec((tm, HEAD_PAD), rope_map), pl.BlockSpec((tm, HEAD_PAD), rope_map),
                pl.BlockSpec((tm, HEAD_PAD), rope_map)]
    out_shapes = [jax.ShapeDtypeStruct((n, hp), BF16), jax.ShapeDtypeStruct((n, hp), BF16),
                  jax.ShapeDtypeStruct((n, MLA_HEADS * V_DIM), BF16), jax.ShapeDtypeStruct((n, NA_WIDTH), BF16),
                  jax.ShapeDtypeStruct((n, NA_WIDTH), F32), jax.ShapeDtypeStruct((n, NA_WIDTH), F32),
                  jax.ShapeDtypeStruct((n, KV_RANK), F32), jax.ShapeDtypeStruct((n, HEAD_PAD), F32)]
    out_specs = [tok(s.shape[1]) for s in out_shapes]
    return pl.pallas_call(
        functools.partial(_attn_in_kernel, ctx_tiles=ctx_tiles),
        grid=(n // tm,),
        in_specs=in_specs,
        out_specs=out_specs,
        out_shape=out_shapes,
        compiler_params=_params(1),
        name="attn_in",
    )(x, g, sc, sh, w["wql"], w["wkvl"], w["wkr"], w["wna"], w["qn"], w["kvn"], w["wqu"], w["wku"], w["wvu"],
      w["qg"], w["kg"], w["nqg"], w["nkg"], *rope)


def _ctx_expand_kernel(ckv_ref, krp_ref, wku_ref, wvu_ref, kg_ref, k_ref, v_ref):
    ckv = ckv_ref[...]
    krp = krp_ref[...]
    k_up = _mm(ckv, wku_ref[...])
    v_ref[...] = _mm(ckv, wvu_ref[...]).astype(v_ref.dtype)
    kg = kg_ref[...]
    for hd in range(MLA_HEADS):
        sl = slice(hd * HEAD_PAD, (hd + 1) * HEAD_PAD)
        k_ref[:, sl] = _head_rms(k_up[:, sl] + krp, kg).astype(k_ref.dtype)


def _ctx_expand_call(ckv, krp, w):
    n = ckv.shape[0]
    tm = min(512, n)
    hp = MLA_HEADS * HEAD_PAD
    return pl.pallas_call(
        _ctx_expand_kernel,
        grid=(n // tm,),
        in_specs=[pl.BlockSpec((tm, KV_RANK), lambda i: (i, 0)), pl.BlockSpec((tm, HEAD_PAD), lambda i: (i, 0)),
                  _full(w["wku"].shape), _full(w["wvu"].shape), _full((1, HEAD_PAD))],
        out_specs=[pl.BlockSpec((tm, hp), lambda i: (i, 0)), pl.BlockSpec((tm, MLA_HEADS * V_DIM), lambda i: (i, 0))],
        out_shape=[jax.ShapeDtypeStruct((n, hp), BF16), jax.ShapeDtypeStruct((n, MLA_HEADS * V_DIM), BF16)],
        compiler_params=_params(1),
        name="ctx_expand",
    )(ckv, krp, w["wku"], w["wvu"], w["kg"])


def _softmax_pv(parts):
    m = functools.reduce(jnp.maximum, [jnp.max(s, axis=-1, keepdims=True) for s, _ in parts])
    acc, den = None, None
    for s, v in parts:
        e = jnp.exp(s - m)
        l = jnp.sum(e, axis=-1, keepdims=True)
        o = _mm(e, v)
        acc = o if acc is None else acc + o
        den = l if den is None else den + l
    return acc / den


def _ctx_attn_kernel(q_ref, k_ref, v_ref, nq_ref, nk_ref, nv_ref, o_ref):
    lo = _lane_iota((q_ref.shape[0], LANES)) < V_DIM
    for j in range(MLA_HEADS // 2):
        vp = v_ref[:, j * LANES:(j + 1) * LANES]
        outs = []
        for hd in (2 * j, 2 * j + 1):
            sl = slice(hd * HEAD_PAD, (hd + 1) * HEAD_PAD)
            s = _mm_nt(q_ref[:, sl], k_ref[:, sl])
            outs.append(_softmax_pv([(s, vp)]))
        o_ref[:, j * LANES:(j + 1) * LANES] = jnp.where(lo, outs[0], outs[1]).astype(o_ref.dtype)
    base = MLA_HEADS * V_DIM
    for j in range(NA_HEADS // 2):
        sl = slice(j * LANES, (j + 1) * LANES)
        qp = nq_ref[:, sl]
        kp = nk_ref[:, sl].astype(BF16)
        vp = nv_ref[:, sl].astype(BF16)
        outs = []
        for half in (0, 1):
            qm = jnp.where(lo if half == 0 else jnp.logical_not(lo), qp, jnp.zeros_like(qp))
            s = _mm_nt(qm, kp)
            outs.append(_softmax_pv([(s, vp)]))
        o_ref[:, base + j * LANES:base + (j + 1) * LANES] = jnp.where(lo, outs[0], outs[1]).astype(o_ref.dtype)


def _ctx_attn_call(lay, q, k, v, nq, nk, nv, out):
    s = lay.seq
    blk = lambda width: pl.BlockSpec((s, width), lambda b: (b, 0))
    return pl.pallas_call(
        _ctx_attn_body,
        grid=(lay.batch,),
        in_specs=[blk(q.shape[1]), blk(k.shape[1]), blk(v.shape[1]), blk(nq.shape[1]), blk(nk.shape[1]),
                  blk(nv.shape[1]), pl.BlockSpec(memory_space=pl.ANY)],
        out_specs=blk(out.shape[1]),
        out_shape=jax.ShapeDtypeStruct(out.shape, out.dtype),
        input_output_aliases={6: 0},
        compiler_params=_params(1),
        name="ctx_attn",
    )(q, k, v, nq, nk, nv, out)


def _ctx_attn_body(q_ref, k_ref, v_ref, nq_ref, nk_ref, nv_ref, alias_ref, o_ref):
    del alias_ref
    _ctx_attn_kernel(q_ref, k_ref, v_ref, nq_ref, nk_ref, nv_ref, o_ref)


def _mla_attn_kernel(q_ref, k_ref, v_ref, kc_ref, vc_ref, alias_ref, o_ref):
    del alias_ref
    lo = _lane_iota((q_ref.shape[0], LANES)) < V_DIM
    for j in range(MLA_HEADS // 2):
        vp = v_ref[:, j * LANES:(j + 1) * LANES]
        vcp = vc_ref[:, j * LANES:(j + 1) * LANES]
        outs = []
        for hd in (2 * j, 2 * j + 1):
            sl = slice(hd * HEAD_PAD, (hd + 1) * HEAD_PAD)
            qh = q_ref[:, sl]
            s1 = _mm_nt(qh, k_ref[:, sl])
            s2 = _mm_nt(qh, kc_ref[:, sl])
            outs.append(_softmax_pv([(s1, vp), (s2, vcp)]))
        o_ref[:, j * LANES:(j + 1) * LANES] = jnp.where(lo, outs[0], outs[1]).astype(o_ref.dtype)


def _mla_attn_call(lay, q, k, v, kc, vc, out):
    L, P = lay.dec_l, lay.past
    tq = min(256, L)
    qb = L // tq
    off_q = lay.n_ctx // tq
    off_l = lay.n_ctx // L
    hp = MLA_HEADS * HEAD_PAD
    vw = MLA_HEADS * V_DIM
    return pl.pallas_call(
        _mla_attn_kernel,
        grid=(lay.dec_b, qb),
        in_specs=[pl.BlockSpec((tq, hp), lambda b, i: (off_q + b * qb + i, 0)),
                  pl.BlockSpec((L, hp), lambda b, i: (off_l + b, 0)),
                  pl.BlockSpec((L, vw), lambda b, i: (off_l + b, 0)),
                  pl.BlockSpec((P, hp), lambda b, i: (b, 0)),
                  pl.BlockSpec((P, vw), lambda b, i: (b, 0)),
                  pl.BlockSpec(memory_space=pl.ANY)],
        out_specs=pl.BlockSpec((tq, vw), lambda b, i: (off_q + b * qb + i, 0)),
        out_shape=jax.ShapeDtypeStruct(out.shape, out.dtype),
        input_output_aliases={5: 0},
        compiler_params=_params(2),
        name="mla_attn",
    )(q, k, v, kc, vc, out)


def _na_bias_kernel(rpb_ref, o_ref):
    h = pl.program_id(0)
    n_dy, n_dx = 2 * NA_KH - 1, 2 * NA_KW - 1
    qc = lax.broadcasted_iota(jnp.int32, (GRID_W, LANES), 0)
    kc = _lane_iota((GRID_W, LANES)) & (GRID_W - 1)
    dx = jnp.clip(kc - qc, -(NA_KW - 1), NA_KW - 1) + NA_KW - 1
    cs = jnp.clip(qc - NA_KW // 2, 0, GRID_W - NA_KW)
    col_ok = (kc >= cs) & (kc < cs + NA_KW)
    tiles = []
    for dy in range(n_dy):
        acc = jnp.zeros((GRID_W, LANES), F32)
        for i in range(n_dx):
            acc = jnp.where(dx == i, rpb_ref[(h * n_dy + dy) * n_dx + i], acc)
        tiles.append(jnp.where(col_ok, acc, NEG_INF))
    zero = jnp.zeros((GRID_W, LANES), F32)
    tiles = [zero] + tiles + [zero]
    lo = _lane_iota((GRID_W, LANES)) < GRID_W
    for i in range(n_dy + 1):
        o_ref[i] = jnp.where(lo, tiles[i], tiles[i + 1])


def _na_bias_call(rpb):
    n_pairs = 2 * NA_KH
    return pl.pallas_call(
        _na_bias_kernel,
        grid=(NA_HEADS,),
        in_specs=[pl.BlockSpec(memory_space=pltpu.SMEM)],
        out_specs=pl.BlockSpec((n_pairs, GRID_W, LANES), lambda h: (h, 0, 0)),
        out_shape=jax.ShapeDtypeStruct((NA_HEADS * n_pairs, GRID_W, LANES), F32),
        compiler_params=_params(1),
        name="na_bias",
    )(rpb.reshape(-1))


def _na_attn_kernel(q_ref, k_ref, v_ref, kc_ref, vc_ref, tp_ref, alias_ref, o_ref, *, rows, win):
    del alias_ref
    g = pl.program_id(1)
    n_pairs = 2 * NA_KH
    nq = NA_QROWS * GRID_W
    nk = win * GRID_W
    ws = jnp.clip(NA_QROWS * g - NA_KH // 2, 0, rows - win)
    start = pl.multiple_of(ws * GRID_W, GRID_W)
    kwin = k_ref[pl.ds(start, nk), :].astype(BF16)
    vwin = v_ref[pl.ds(start, nk), :].astype(BF16)
    kctx = kc_ref[...].astype(BF16)
    vctx = vc_ref[...].astype(BF16)

    qr = NA_QROWS * g + (lax.broadcasted_iota(jnp.int32, (nq, nk), 0) >> GRID_SHIFT)
    kr = ws + (_lane_iota((nq, nk)) >> GRID_SHIFT)
    rs = jnp.clip(qr - NA_KH // 2, 0, rows - NA_KH)
    row_ok = (kr >= rs) & (kr < rs + NA_KH)

    lo = _lane_iota((nq, LANES)) < NA_HEAD_DIM
    for j in range(NA_HEADS // 2):
        sl = slice(j * LANES, (j + 1) * LANES)
        qp = q_ref[:, sl]
        kp, vp, kcp, vcp = kwin[:, sl], vwin[:, sl], kctx[:, sl], vctx[:, sl]
        outs = []
        for half in (0, 1):
            hd = 2 * j + half
            bias_rows = []
            for a in range(NA_QROWS):
                blocks = []
                for p in range(win // 2):
                    dy = ws + 2 * p - (NA_QROWS * g + a) + NA_KH - 1
                    idx = jnp.clip(dy + 1, 0, n_pairs - 1)
                    blocks.append(tp_ref[hd * n_pairs + idx])
                bias_rows.append(jnp.concatenate(blocks, axis=1))
            bias = jnp.concatenate(bias_rows, axis=0)
            qm = jnp.where(lo if half == 0 else jnp.logical_not(lo), qp, jnp.zeros_like(qp))
            s_w = jnp.where(row_ok, _mm_nt(qm, kp) + bias, NEG_INF)
            s_c = _mm_nt(qm, kcp)
            outs.append(_softmax_pv([(s_w, vp), (s_c, vcp)]))
        o_ref[:, sl] = jnp.where(lo, outs[0], outs[1]).astype(o_ref.dtype)


def _na_attn_call(lay, nq, nk, nv, kc, vc, tp, out):
    L, P = lay.dec_l, lay.past
    rows = L // GRID_W
    assert rows % NA_QROWS == 0
    win = min(2 * NA_KH, rows)
    groups = rows // NA_QROWS
    tq = NA_QROWS * GRID_W
    off_q = lay.n_ctx // tq
    off_l = lay.n_ctx // L
    n_out_blk = out.shape[1] // NA_WIDTH
    return pl.pallas_call(
        functools.partial(_na_attn_kernel, rows=rows, win=win),
        grid=(lay.dec_b, groups),
        in_specs=[pl.BlockSpec((tq, NA_WIDTH), lambda b, g: (off_q + b * groups + g, 0)),
                  pl.BlockSpec((L, NA_WIDTH), lambda b, g: (off_l + b, 0)),
                  pl.BlockSpec((L, NA_WIDTH), lambda b, g: (off_l + b, 0)),
                  pl.BlockSpec((P, NA_WIDTH), lambda b, g: (b, 0)),
                  pl.BlockSpec((P, NA_WIDTH), lambda b, g: (b, 0)),
                  _full(tp.shape),
                  pl.BlockSpec(memory_space=pl.ANY)],
        out_specs=pl.BlockSpec((tq, NA_WIDTH), lambda b, g: (off_q + b * groups + g, n_out_blk - 1)),
        out_shape=jax.ShapeDtypeStruct(out.shape, out.dtype),
        input_output_aliases={6: 0},
        compiler_params=_params(2),
        name="na_attn",
    )(nq, nk, nv, kc, vc, tp, out)


def _hy_filter_kernel(z_ref, w1_ref, b1_ref, fr_ref, w2_ref, b2_ref, w3a_ref, w3b_ref, w3c_ref, w3d_ref,
                      dl_ref, t_ref, alt_ref, wf_ref, c_ref, s_ref, kr_ref, ki_ref, kl_ref):
    h = jnp.sin(fr_ref[0:1, :] * (_mm_hi(z_ref[...], w1_ref[...]) + b1_ref[...]))
    h = jnp.sin(fr_ref[1:2, :] * (_mm_hi(h, w2_ref[...]) + b2_ref[...]))
    window = jnp.exp(-t_ref[...] * dl_ref[...]) + HY_MOD_SHIFT
    first = lax.broadcasted_iota(jnp.int32, window.shape, 0) == 0
    alt = alt_ref[...]
    wf = wf_ref[...]
    fwd = (w3a_ref, w3b_ref)
    bwd = (w3c_ref, w3d_ref)
    for o in range(HY_ORDER):
        hf = _mm_hi(h, fwd[o][...]) * window
        hb = jnp.where(first, 0.0, _mm_hi(h, bwd[o][...]) * window)
        norm = jnp.sum(jnp.abs(hf), axis=0, keepdims=True) + jnp.sum(jnp.abs(hb), axis=0, keepdims=True) + EPS
        a = (hf + hb) / norm
        d = (hf - hb) / norm
        kl_ref[o] = jnp.sum(a * alt, axis=0, keepdims=True) * (0.5 / a.shape[0])
        a, d = a.astype(BF16), d.astype(BF16)
        n_rows = a.shape[0]
        ch = min(DFT_ROWS, n_rows)
        for i in range(n_rows // ch):
            rows = slice(i * ch, (i + 1) * ch)
            kr_ref[o, rows, :] = jnp.dot(c_ref[rows, :], a, preferred_element_type=F32) * wf[rows]
            ki_ref[o, rows, :] = -jnp.dot(s_ref[rows, :], d, preferred_element_type=F32) * wf[rows]


def _hy_filter_call(L, z, p, consts):
    d = D_MODEL
    dc = 256
    nc = d // dc
    w3 = p["w3"]
    w3_spec = lambda k: pl.BlockSpec((HY_FF, dc), lambda c: (0, k * nc + c))
    zpad = z.shape[1]
    return pl.pallas_call(
        _hy_filter_kernel,
        grid=(nc,),
        in_specs=[_full((L, zpad)), _full((zpad, HY_FF)), _full((1, HY_FF)), _full((2, HY_FF)),
                  _full((HY_FF, HY_FF)), _full((1, HY_FF)), w3_spec(0), w3_spec(1), w3_spec(2), w3_spec(3),
                  pl.BlockSpec((1, dc), lambda c: (0, c)), _full((L, 1)), _full((L, 1)), _full((L, 1)),
                  _full((L, L)), _full((L, L))],
        out_specs=[pl.BlockSpec((HY_ORDER, L, dc), lambda c: (0, 0, c)),
                   pl.BlockSpec((HY_ORDER, L, dc), lambda c: (0, 0, c)),
                   pl.BlockSpec((HY_ORDER, 1, dc), lambda c: (0, 0, c))],
        out_shape=[jax.ShapeDtypeStruct((HY_ORDER, L, d), F32), jax.ShapeDtypeStruct((HY_ORDER, L, d), F32),
                   jax.ShapeDtypeStruct((HY_ORDER, 1, d), F32)],
        compiler_params=_params(1),
        name="hy_filter",
    )(z, p["w1"], p["b1"], p["freq"], p["w2"], p["b2"], w3, w3, w3, w3,
      consts["deltas"], consts["t"], consts["alt"], consts["wf"], consts["C"], consts["S"])


def _hy_stage_kernel(u_ref, g_ref, cwu_ref, cbu_ref, cwg_ref, cbg_ref, kr_ref, ki_ref, kl_ref, skip_ref, alt_ref,
                     c_ref, s_ref, *out_refs, conv_u):
    o_ref = out_refs[-1]
    L = u_ref.shape[0]
    row = lax.broadcasted_iota(jnp.int32, u_ref.shape, 0)
    first, last = row == 0, row == L - 1

    def short_conv(ref, w_ref, b_ref):
        x = ref[...]
        prev = jnp.where(first, 0.0, pltpu.roll(x, 1, 0))
        nxt = jnp.where(last, 0.0, pltpu.roll(x, L - 1, 0))
        w = w_ref[...]
        return prev * w[0:1] + x * w[1:2] + nxt * w[2:3] + b_ref[...]

    u = short_conv(u_ref, cwu_ref, cbu_ref) if conv_u else u_ref[...]
    gate = short_conv(g_ref, cwg_ref, cbg_ref)
    alt = alt_ref[...]
    ub = u.astype(BF16)
    ul_kl = jnp.sum(u * alt, axis=0, keepdims=True) * kl_ref[...]
    skip = skip_ref[...]
    ch = min(DFT_ROWS, L)
    yr, zi = [], []
    for i in range(L // ch):
        rows = slice(i * ch, (i + 1) * ch)
        ur = jnp.dot(c_ref[rows, :], ub, preferred_element_type=F32)
        us = jnp.dot(s_ref[rows, :], ub, preferred_element_type=F32)
        kr, ki = kr_ref[rows, :], ki_ref[rows, :]
        yr.append((ur * kr + us * ki).astype(BF16))
        zi.append((us * kr - ur * ki).astype(BF16))
    yr = jnp.concatenate(yr, axis=0)
    zi = jnp.concatenate(zi, axis=0)
    for i in range(L // ch):
        rows = slice(i * ch, (i + 1) * ch)
        y = jnp.dot(c_ref[rows, :], yr, preferred_element_type=F32) + jnp.dot(s_ref[rows, :], zi,
                                                                              preferred_element_type=F32)
        y = y + alt[rows] * ul_kl + u[rows] * skip
        o_ref[rows, :] = (gate[rows] * y).astype(o_ref.dtype)


def _hy_stage_call(u, u_blk, gate, g_blk, cwu, cbu, cwg, cbg, kr, ki, kl, skip, consts, *, conv_u, L, nseq,
                   row_off, out, td):
    d = D_MODEL
    nc = d // td
    off = row_off // L
    one = pl.Buffered(1)
    chan = lambda rows: pl.BlockSpec((rows, td), lambda c, b: (0, c))
    args = [u, gate, cwu, cbu, cwg, cbg, kr, ki, kl, skip, consts["alt"], consts["C"], consts["S"]]
    in_specs = [pl.BlockSpec((L, td), lambda c, b: (off + b, u_blk * nc + c)),
                pl.BlockSpec((L, td), lambda c, b: (off + b, g_blk * nc + c)),
                chan(3), chan(1), chan(3), chan(1),
                pl.BlockSpec((L, td), lambda c, b: (0, c), pipeline_mode=one),
                pl.BlockSpec((L, td), lambda c, b: (0, c), pipeline_mode=one),
                chan(1), chan(1),
                _full((L, 1)),
                pl.BlockSpec((L, L), lambda c, b: (0, 0), pipeline_mode=one),
                pl.BlockSpec((L, L), lambda c, b: (0, 0), pipeline_mode=one)]
    aliases = {}
    if not isinstance(out, jax.ShapeDtypeStruct):
        aliases = {len(args): 0}
        args.append(out)
        in_specs.append(pl.BlockSpec(memory_space=pl.ANY))
    return pl.pallas_call(
        functools.partial(_hy_stage_kernel, conv_u=conv_u),
        grid=(nc, nseq),
        in_specs=in_specs,
        out_specs=pl.BlockSpec((L, td), lambda c, b: (off + b, c)),
        out_shape=jax.ShapeDtypeStruct(out.shape, out.dtype),
        input_output_aliases=aliases,
        compiler_params=_params(2),
        name="hy_stage_%d" % L,
    )(*args)


def _dft_consts(L):
    f = jnp.arange(L, dtype=jnp.int32)
    ang = ((f[:, None] * f[None, :]) % (2 * L)).astype(F32) * (math.pi / L)
    t = jnp.linspace(0.0, 1.0, L, dtype=F32)[:, None]
    w = 2.0 * math.pi * jnp.arange(L, dtype=F32)[:, None] / L
    bands = jnp.linspace(1e-4, HY_BANDS - 1, HY_BANDS, dtype=F32)[None, :]
    z = jnp.concatenate([t, jnp.cos(bands * w), -jnp.sin(bands * w)], axis=-1)
    z = jnp.pad(z, ((0, 0), (0, 32 - HY_EMB)))
    deltas = jnp.abs(jnp.linspace(math.log(HY_TARGET) / HY_FAST_DECAY, math.log(HY_TARGET) / HY_SLOW_DECAY,
                                  D_MODEL, dtype=F32))[None, :]
    alt = (1.0 - 2.0 * (f % 2).astype(F32))[:, None]
    wf = jnp.where(f == 0, 0.5 / L, 1.0 / L).astype(F32)[:, None]
    return {"C": jnp.cos(ang).astype(BF16), "S": jnp.sin(ang).astype(BF16), "t": t, "z": z, "deltas": deltas,
            "alt": alt, "wf": wf}


def _pack_bf16_pairs(x):
    half = x.shape[1] // 2
    bits = lax.bitcast_convert_type(x.astype(BF16).astype(F32), jnp.uint32)
    return (bits[:, half:] & jnp.uint32(0xFFFF0000)) | (bits[:, :half] >> 16)


def _unpack_bf16_pairs(p):
    lo = lax.bitcast_convert_type(p << 16, F32)
    hi = lax.bitcast_convert_type(p & jnp.uint32(0xFFFF0000), F32)
    return jnp.concatenate([lo, hi], axis=1).astype(BF16)


def _route_kernel(x_ref, g_ref, sc_ref, sh_ref, wr_ref, br_ref, h_ref, meta_ref, cnt_ref, base_ref):
    i = pl.program_id(0)
    tm = x_ref.shape[0]

    @pl.when(i == 0)
    def _():
        base_ref[...] = jnp.zeros_like(base_ref)

    h = _norm_mod(x_ref[...], g_ref[...], sc_ref[...], sh_ref[...])
    packed = _pack_bf16_pairs(h)
    for s in range(PACK_ROWS):
        h_ref[pl.ds(s, x_ref.shape[0], stride=PACK_ROWS), :] = packed[:, s * LANES:(s + 1) * LANES]
    logits = _mm_hi(h, wr_ref[...]) + br_ref[...]
    lane_i = _lane_iota(logits.shape)
    lane = lane_i.astype(F32)
    grp_of_lane = (lane_i >> 3).astype(F32)
    big = 1e6
    is_g = (lane_i >= N_EXPERTS) & (lane_i < N_EXPERTS + N_GROUPS)
    glog = jnp.where(is_g, logits, NEG_INF)
    gmax = jnp.max(glog, axis=-1, keepdims=True)
    g_w = 1.0 / jnp.sum(jnp.exp(glog - gmax), axis=-1, keepdims=True)
    g_i = jnp.min(jnp.where(glog == gmax, lane - N_EXPERTS, big), axis=-1, keepdims=True)
    in_grp = (lane_i < N_EXPERTS) & (grp_of_lane == g_i)
    elog = jnp.where(in_grp, logits, NEG_INF)
    m1 = jnp.max(elog, axis=-1, keepdims=True)
    i1 = jnp.min(jnp.where(elog == m1, lane, big), axis=-1, keepdims=True)
    elog2 = jnp.where(lane == i1, NEG_INF, elog)
    m2 = jnp.max(elog2, axis=-1, keepdims=True)
    i2 = jnp.min(jnp.where(elog2 == m2, lane, big), axis=-1, keepdims=True)
    a2 = jnp.exp(m2 - m1)
    w1 = g_w / (1.0 + a2)
    w2 = g_w * a2 / (1.0 + a2)

    oh1 = jnp.where(lane == i1, 1.0, 0.0).astype(BF16)
    oh2 = jnp.where(lane == i2, 1.0, 0.0).astype(BF16)
    tri = jnp.where(lax.broadcasted_iota(jnp.int32, (tm, tm), 0) > lax.broadcasted_iota(jnp.int32, (tm, tm), 1),
                    1.0, 0.0).astype(BF16)
    cum1 = jnp.dot(tri, oh1, preferred_element_type=F32)
    cum2 = jnp.dot(tri, oh2, preferred_element_type=F32)
    tot1 = jnp.sum(oh1.astype(F32), axis=0, keepdims=True)
    tot2 = jnp.sum(oh2.astype(F32), axis=0, keepdims=True)
    base = base_ref[...]
    r1 = jnp.sum(jnp.where(lane == i1, base + cum1, 0.0), axis=-1, keepdims=True)
    r2 = jnp.sum(jnp.where(lane == i2, base + tot1 + cum2, 0.0), axis=-1, keepdims=True)
    base = base + tot1 + tot2
    base_ref[...] = base
    cnt_ref[...] = base
    cols = [i1, i2, r1, r2, w1, w2]
    meta = jnp.zeros(logits.shape, F32)
    for c, val in enumerate(cols):
        meta = jnp.where(lane_i == c, val, meta)
    meta_ref[...] = meta


def _route_call(lay, x, norm, wr, br):
    n, d = x.shape
    tm = lay.tm
    g, sc, sh = norm
    tok = lambda width: pl.BlockSpec((tm, width), lambda i: (i, 0))
    return pl.pallas_call(
        _route_kernel,
        grid=(n // tm,),
        in_specs=[tok(d), _full((1, d)), lay.mod_spec(), lay.mod_spec(), _full((d, LANES)), _full((1, LANES))],
        out_specs=[pl.BlockSpec((tm * PACK_ROWS, LANES), lambda i: (i, 0)), tok(LANES), _full((1, LANES))],
        out_shape=[jax.ShapeDtypeStruct((n * PACK_ROWS, LANES), jnp.uint32), jax.ShapeDtypeStruct((n, LANES), F32),
                   jax.ShapeDtypeStruct((1, LANES), F32)],
        scratch_shapes=[pltpu.VMEM((1, LANES), F32)],
        compiler_params=_params(1),
        name="moe_route",
    )(x, g, sc, sh, wr, br)


SUBLANES = 8
PACK_ROWS = 4


def _as_tiles(x_ref, rows):
    return jnp.concatenate([x_ref[pl.ds(s, rows, stride=SUBLANES), :] for s in range(SUBLANES)], axis=1)


def _tile_copy(src_ref, s, dst_ref, d8, sem):
    return pltpu.make_async_copy(src_ref.at[pl.ds(s * SUBLANES, SUBLANES)],
                                 dst_ref.at[pl.ds(pl.multiple_of(d8, SUBLANES), SUBLANES)], sem)


def _expert_kernel(te_ref, src_ref, nxt_ref, dst_ref, prv_ref, h_ref, wg_ref, wu_ref, wd_ref, o_ref,
                   xbuf0, xbuf1, ybuf0, ybuf1, zbuf, sem_out, sem_z, *, trash_base):
    j = pl.program_id(0)
    tiles = pl.num_programs(0)
    used = te_ref[tiles]
    parity = lax.rem(j, 2)
    tm = ybuf0.shape[0] // SUBLANES
    xbufs, ybufs = (xbuf0, xbuf1), (ybuf0, ybuf1)

    def gather(idx_ref, buf):
        for r in range(tm):
            buf[r * PACK_ROWS:(r + 1) * PACK_ROWS, :] = h_ref[pl.ds(pl.multiple_of(idx_ref[0, r], PACK_ROWS),
                                                                    PACK_ROWS), :]

    def start_scatter(p, idx_ref):
        for r in range(tm):
            _tile_copy(ybufs[p], r, o_ref, idx_ref[0, r], sem_out.at[p]).start(priority=r % 2)

    def wait_scatter(p):
        for r in range(tm):
            _tile_copy(ybufs[p], r, o_ref, 0, sem_out.at[p]).wait()

    @pl.when(j == 0)
    def _():
        zbuf[...] = jnp.zeros_like(zbuf)
        rows = tm * SUBLANES
        fills = [pltpu.make_async_copy(zbuf, o_ref.at[pl.ds((trash_base + k * tm) * SUBLANES, rows)], sem_z)
                 for k in range(N_EXPERTS)]
        for f in fills:
            f.start()
        gather(src_ref, xbuf0)
        for f in fills:
            f.wait()

    def step(p, first):
        if not first:
            @pl.when(j >= 2)
            def _():
                wait_scatter(p)

        gather(nxt_ref, xbufs[1 - p])
        x = _unpack_bf16_pairs(jnp.concatenate(
            [xbufs[p][pl.ds(s, tm, stride=PACK_ROWS), :] for s in range(PACK_ROWS)], axis=1))
        hg = _mm(x, wg_ref[...])
        hu = _mm(x, wu_ref[...])
        act = hg * jax.nn.sigmoid(hg) * hu
        y = _mm(act, wd_ref[...])
        if not first:
            start_scatter(1 - p, prv_ref)
        for s in range(SUBLANES):
            ybufs[p][pl.ds(s, tm, stride=SUBLANES), :] = y[:, s * LANES:(s + 1) * LANES]

        @pl.when(j == used - 1)
        def _():
            start_scatter(p, dst_ref)
            wait_scatter(p)
            if not first:
                wait_scatter(1 - p)

    pl.when((j < used) & (j == 0))(functools.partial(step, 0, True))
    for p in (0, 1):
        pl.when((j < used) & (j > 0) & (parity == p))(functools.partial(step, p, False))


def _expert_call(layer, tile_expert, src, dst, h, wg, wu, wd):
    n = h.shape[0] // PACK_ROWS
    d = 2 * PACK_ROWS * LANES
    f = wg.shape[-1]
    tiles = src.shape[0]
    tm = MOE_TILE
    idx = pl.BlockSpec((None, 1, tm), lambda j, te: (j, 0, 0), memory_space=pltpu.SMEM)
    idx_next = pl.BlockSpec((None, 1, tm), lambda j, te: (jnp.minimum(j + 1, tiles - 1), 0, 0),
                            memory_space=pltpu.SMEM)
    idx_prev = pl.BlockSpec((None, 1, tm), lambda j, te: (jnp.maximum(j - 1, 0), 0, 0), memory_space=pltpu.SMEM)
    wspec = lambda a, b: pl.BlockSpec((None, None, a, b), lambda j, te: (layer, te[j], 0, 0))
    ytile = pltpu.VMEM((tm * SUBLANES, LANES), F32)
    grid_spec = pltpu.PrefetchScalarGridSpec(
        num_scalar_prefetch=1,
        grid=(tiles,),
        in_specs=[idx, idx_next, idx, idx_prev,
                  pl.BlockSpec(h.shape, lambda j, te: (0, 0), pipeline_mode=pl.Buffered(1)),
                  wspec(d, f), wspec(d, f), wspec(f, d)],
        out_specs=pl.BlockSpec(memory_space=pl.ANY),
        scratch_shapes=[pltpu.VMEM((tm * PACK_ROWS, LANES), jnp.uint32),
                        pltpu.VMEM((tm * PACK_ROWS, LANES), jnp.uint32), ytile, ytile, ytile,
                        pltpu.SemaphoreType.DMA((2,)), pltpu.SemaphoreType.DMA(())],
    )
    assert d == SUBLANES * LANES
    return pl.pallas_call(
        functools.partial(_expert_kernel, trash_base=2 * n),
        grid_spec=grid_spec,
        out_shape=jax.ShapeDtypeStruct(((2 * n + N_EXPERTS * tm) * SUBLANES, LANES), F32),
        compiler_params=_params(1),
        name="moe_experts",
    )(tile_expert, src * PACK_ROWS, src * PACK_ROWS, dst * SUBLANES, dst * SUBLANES, h, wg, wu, wd)


def _plan_kernel(pos_ref, fill_ref, inv_ref, sem):
    i = pl.program_id(0)
    ch = pos_ref.shape[1]

    @pl.when(i == 0)
    def _():
        clear = pltpu.make_async_copy(fill_ref, inv_ref, sem)
        clear.start()
        clear.wait()

    def place(t, c):
        inv_ref[pos_ref[0, t]] = i * ch + t
        return c

    lax.fori_loop(0, ch, place, 0, unroll=16)


def _plan_call(pos, rows):
    total = pos.shape[0]
    ch = min(4096, total)
    assert total % ch == 0
    return pl.pallas_call(
        _plan_kernel,
        grid=(total // ch,),
        in_specs=[pl.BlockSpec((None, 1, ch), lambda i: (i, 0, 0), memory_space=pltpu.SMEM),
                  pl.BlockSpec(memory_space=pl.ANY)],
        out_specs=pl.BlockSpec(memory_space=pltpu.SMEM),
        out_shape=jax.ShapeDtypeStruct((rows,), jnp.int32),
        scratch_shapes=[pltpu.SemaphoreType.DMA(())],
        compiler_params=_params(1),
        name="moe_plan",
    )(pos.reshape(total // ch, 1, ch), jnp.full((rows,), -1, jnp.int32))


def _combine_kernel(x_ref, gate_ref, meta_ref, y0_ref, y1_ref, o_ref):
    meta = meta_ref[...]
    tm = x_ref.shape[0]
    y = meta[:, 4:5] * _as_tiles(y0_ref, tm) + meta[:, 5:6] * _as_tiles(y1_ref, tm)
    o_ref[...] = x_ref[...] + gate_ref[...] * y


def _combine_call(lay, x, gate, meta, ys):
    n, d = x.shape
    tm = lay.tm
    tok = lambda width: pl.BlockSpec((tm, width), lambda i: (i, 0))
    rows = lambda off: pl.BlockSpec((tm * SUBLANES, LANES), lambda i: (off + i, 0))
    return pl.pallas_call(
        _combine_kernel,
        grid=(n // tm,),
        in_specs=[tok(d), lay.mod_spec(), tok(LANES), rows(0), rows(n // tm)],
        out_specs=tok(d),
        out_shape=jax.ShapeDtypeStruct((n, d), F32),
        compiler_params=_params(1),
        name="moe_combine",
    )(x, gate, meta, ys, ys)


def _moe_layer(lay, layer, x, norm, gate, wr, br, wg, wu, wd):
    n = x.shape[0]
    tm = MOE_TILE
    h, meta, counts = _route_call(lay, x, norm, wr, br)
    cnt = counts[0, :N_EXPERTS].astype(jnp.int32)
    padded = ((cnt + tm - 1) // tm) * tm
    ends = jnp.cumsum(padded)
    starts = ends - padded
    m4 = meta[:, 0:4].T.astype(jnp.int32)
    onehot = m4[0:2, :, None] == jnp.arange(N_EXPERTS, dtype=jnp.int32)[None, None, :]
    pos = jnp.sum(jnp.where(onehot, starts[None, None, :], 0), axis=-1) + m4[2:4]
    rows = 2 * n + N_EXPERTS * tm
    tiles = rows // tm
    inv = _plan_call(pos.reshape(-1), rows)
    is_pad = inv < 0
    trash = 2 * n - 1 + jnp.cumsum(is_pad.astype(jnp.int32))
    dst = jnp.where(is_pad, trash, inv).reshape(tiles, 1, tm)
    src = jnp.where(is_pad, 0, jnp.where(inv >= n, inv - n, inv)).reshape(tiles, 1, tm)
    tile_start = jnp.arange(tiles, dtype=jnp.int32) * tm
    tile_expert = jnp.sum((tile_start[:, None] >= ends[None, :]).astype(jnp.int32), axis=1)
    used = ends[-1] // tm
    last = jnp.take(tile_expert, jnp.maximum(used - 1, 0))
    tile_expert = jnp.where(tile_start // tm < used, tile_expert, last)
    tile_expert = jnp.concatenate([tile_expert, used[None]]).astype(jnp.int32)
    ys = _expert_call(layer, tile_expert, src, dst, h, wg, wu, wd)
    return _combine_call(lay, x, gate, meta, ys)


def _attn_weights(a, attn_w_in, mla_q_norm, mla_kv_norm, mla_w_q_up, mla_w_kv_up, mla_q_gain, mla_k_gain,
                  na_q_gain, na_k_gain):
    w_in = attn_w_in[a]
    o1, o2, o3 = Q_RANK, Q_RANK + KV_RANK, Q_RANK + KV_RANK + ROPE_DIM
    pad_head = lambda g: jnp.pad(g, (0, HEAD_PAD - QK_DIM))[None, :]
    wkr = jnp.zeros((D_MODEL, HEAD_PAD), F32).at[:, NOPE_DIM:QK_DIM].set(w_in[:, o2:o3])
    wqu = jnp.pad(mla_w_q_up[a].reshape(Q_RANK, MLA_HEADS, QK_DIM), ((0, 0), (0, 0), (0, HEAD_PAD - QK_DIM)))
    wkv = mla_w_kv_up[a].reshape(KV_RANK, MLA_HEADS, NOPE_DIM + V_DIM)
    wku = jnp.pad(wkv[:, :, :NOPE_DIM], ((0, 0), (0, 0), (0, HEAD_PAD - NOPE_DIM)))
    return {
        "wql": w_in[:, :o1].astype(BF16), "wkvl": w_in[:, o1:o2].astype(BF16), "wkr": wkr.astype(BF16),
        "wna": w_in[:, o3:].astype(BF16),
        "qn": mla_q_norm[a][None, :], "kvn": mla_kv_norm[a][None, :],
        "wqu": wqu.reshape(Q_RANK, MLA_HEADS * HEAD_PAD).astype(BF16),
        "wku": wku.reshape(KV_RANK, MLA_HEADS * HEAD_PAD).astype(BF16),
        "wvu": wkv[:, :, NOPE_DIM:].reshape(KV_RANK, MLA_HEADS * V_DIM).astype(BF16),
        "qg": pad_head(mla_q_gain[a]) * MLA_SCALE, "kg": pad_head(mla_k_gain[a]),
        "nqg": jnp.tile(na_q_gain[a], LANES // NA_HEAD_DIM)[None, :] * NA_SCALE,
        "nkg": jnp.tile(na_k_gain[a], LANES // NA_HEAD_DIM)[None, :],
    }


def _rope_tables(L):
    t = jnp.arange(L)
    quarter = ROPE_DIM // 4
    inv = ROPE_THETA ** (-jnp.arange(quarter, dtype=F32) / quarter)
    ang_r = (t // GRID_W).astype(F32)[:, None] * inv[None, :]
    ang_c = (t % GRID_W).astype(F32)[:, None] * inv[None, :]
    zeros = jnp.zeros((L, quarter), F32)
    ones_nope = jnp.ones((L, NOPE_DIM), F32)
    pad = jnp.zeros((L, HEAD_PAD - QK_DIM), F32)
    cos = jnp.concatenate([ones_nope, jnp.cos(ang_r), jnp.cos(ang_r), jnp.cos(ang_c), jnp.cos(ang_c), pad], axis=1)
    s_lo = jnp.concatenate([0 * ones_nope, -jnp.sin(ang_r), zeros, -jnp.sin(ang_c), zeros, pad], axis=1)
    s_hi = jnp.concatenate([0 * ones_nope, zeros, jnp.sin(ang_r), zeros, jnp.sin(ang_c), pad], axis=1)
    return cos, s_lo, s_hi


def kernel(x_prompt, x_sample, cache_mla_ckv, cache_mla_krope, cache_na_k, cache_na_v, c, c_ctx, ada_w, ada_b, norm_mix, norm_ffn, attn_w_in, mla_q_norm, mla_kv_norm, mla_w_q_up, mla_w_kv_up, mla_q_gain, mla_k_gain, na_q_gain, na_k_gain, na_rpb, attn_w_out, hy_w_in, hy_b_in, hy_conv_w, hy_conv_b, hy_filt_w1, hy_filt_b1, hy_filt_freq, hy_filt_w2, hy_filt_b2, hy_filt_w3, hy_skip, hy_w_out, moe_w_group, moe_b_group, moe_w_router, moe_b_router, moe_w_gate, moe_w_up, moe_w_down):
    batch, seq, d = x_prompt.shape
    dec_b, dec_l, _ = x_sample.shape
    past = cache_mla_ckv.shape[2]
    depth = ada_w.shape[0]
    lay = _Layout(batch, seq, dec_b, dec_l, past)
    n_ctx = lay.n_ctx

    x = jnp.concatenate([x_prompt.reshape(n_ctx, d), x_sample.reshape(lay.n_dec, d)], axis=0)
    cvec = jnp.concatenate([c_ctx[None, :], c], axis=0)
    cvec_t = jnp.pad(cvec, ((0, MOD_ROWS - cvec.shape[0]), (0, 0))).T
    mods = _ada_call(cvec_t, ada_w, ada_b, 1 + dec_b)

    def mod(l, j):
        return mods[l, :, j * d:(j + 1) * d].reshape(MOD_ROWS, 1, d)

    rope = _rope_tables(dec_l)
    consts = {L: _dft_consts(L) for L in sorted({seq, dec_l})}
    ckv_out, krope_out, nk_out, nv_out = [], [], [], []

    for l in range(depth):
        sh1, sc1, g1, sh2, sc2, g2 = [mod(l, j) for j in range(6)]
        norm1 = (norm_mix[l][None, :], sc1, sh1)
        if l % 2 == 0:
            a = l // 2
            w = _attn_weights(a, attn_w_in, mla_q_norm, mla_kv_norm, mla_w_q_up, mla_w_kv_up, mla_q_gain,
                              mla_k_gain, na_q_gain, na_k_gain)
            q, k, v, nq, nk, nv, ckv, krp = _attn_in_call(lay, x, norm1, w, rope)
            ckv_out.append(ckv[:n_ctx].reshape(batch, seq, KV_RANK))
            krope_out.append(krp[:n_ctx, NOPE_DIM:QK_DIM].reshape(batch, seq, ROPE_DIM))
            nk_out.append(nk[:n_ctx].reshape(batch, seq, NA_HEADS, NA_HEAD_DIM))
            nv_out.append(nv[:n_ctx].reshape(batch, seq, NA_HEADS, NA_HEAD_DIM))
            krp_c = jnp.pad(cache_mla_krope[:, a].reshape(dec_b * past, ROPE_DIM),
                            ((0, 0), (NOPE_DIM, HEAD_PAD - QK_DIM)))
            kc, vc = _ctx_expand_call(cache_mla_ckv[:, a].reshape(dec_b * past, KV_RANK), krp_c, w)
            tp = _na_bias_call(na_rpb[a])
            att = jnp.zeros((lay.n, MLA_HEADS * V_DIM + NA_WIDTH), BF16)
            att = _ctx_attn_call(lay, q, k, v, nq, nk, nv, att)
            att = _mla_attn_call(lay, q, k, v, kc, vc, att)
            att = _na_attn_call(lay, nq, nk, nv, cache_na_k[:, a].reshape(dec_b * past, NA_WIDTH),
                                cache_na_v[:, a].reshape(dec_b * past, NA_WIDTH), tp, att)
            x = _dense_call(lay, att, attn_w_out[a].astype(BF16), resid=(x, g1), name="attn_out")
        else:
            j = l // 2
            p3 = _dense_call(lay, x, hy_w_in[j].astype(BF16), norm=norm1, bias=hy_b_in[j][None, :], name="hy_in")
            cw = hy_conv_w[j].reshape(3, 3, d).transpose(1, 0, 2)
            cb = hy_conv_b[j].reshape(3, 1, d)
            skip = hy_skip[j].reshape(HY_ORDER, 1, d)
            filt = {"w1": jnp.pad(hy_filt_w1[j], ((0, 32 - HY_EMB), (0, 0))), "b1": hy_filt_b1[j][None, :],
                    "freq": hy_filt_freq[j], "w2": hy_filt_w2[j], "b2": hy_filt_b2[j][None, :],
                    "w3": hy_filt_w3[j]}
            z1 = jnp.zeros((lay.n, d), F32)
            z = jnp.zeros((lay.n, d), BF16)
            for L, nseq, off, td in ((seq, batch, 0, d), (dec_l, dec_b, n_ctx, 256)):
                kr, ki, kl = _hy_filter_call(L, consts[L]["z"], filt, consts[L])
                geom = dict(L=L, nseq=nseq, row_off=off, td=min(td, d))
                z1 = _hy_stage_call(p3, 0, p3, 1, cw[0], cb[0], cw[1], cb[1], kr[0], ki[0], kl[0], skip[0],
                                    consts[L], conv_u=True, out=z1, **geom)
                z = _hy_stage_call(z1, 0, p3, 2, cw[0], cb[0], cw[2], cb[2], kr[1], ki[1], kl[1], skip[1],
                                   consts[L], conv_u=False, out=z, **geom)
            x = _dense_call(lay, z, hy_w_out[j].astype(BF16), resid=(x, g1), name="hy_out")
        wr = jnp.zeros((d, LANES), F32).at[:, :N_EXPERTS].set(moe_w_router[l])
        wr = wr.at[:, N_EXPERTS:N_EXPERTS + N_GROUPS].set(moe_w_group[l])
        br = jnp.zeros((1, LANES), F32).at[0, :N_EXPERTS].set(moe_b_router[l])
        br = br.at[0, N_EXPERTS:N_EXPERTS + N_GROUPS].set(moe_b_group[l])
        x = _moe_layer(lay, l, x, (norm_ffn[l][None, :], sc2, sh2), g2, wr, br, moe_w_gate, moe_w_up, moe_w_down)

    y_prompt = x[:n_ctx].reshape(batch, seq, d)
    y_sample = x[n_ctx:].reshape(dec_b, dec_l, d)
    return (y_prompt, y_sample, jnp.stack(ckv_out, axis=1), jnp.stack(krope_out, axis=1),
            jnp.stack(nk_out, axis=1), jnp.stack(nv_out, axis=1))
```

```python
import functools
import math

import jax
import jax.numpy as jnp
import numpy as np
from jax import lax
from jax.experimental import pallas as pl
from jax.experimental.pallas import tpu as pltpu

F32 = jnp.float32
BF16 = jnp.bfloat16
HIGHEST = lax.Precision.HIGHEST

D_MODEL = 1024
DEPTH = 4
GRID_W = 64
GRID_SHIFT = 6
MLA_HEADS = 8
Q_RANK = 384
KV_RANK = 128
NOPE_DIM = 64
ROPE_DIM = 32
V_DIM = 64
QK_DIM = NOPE_DIM + ROPE_DIM
ROPE_THETA = 10000.0
NA_HEADS = 8
NA_HEAD_DIM = 64
NA_KH = 8
NA_KW = 16
NA_WIDTH = NA_HEADS * NA_HEAD_DIM
HY_ORDER = 2
HY_BANDS = 8
HY_EMB = 1 + 2 * HY_BANDS
HY_FF = 64
HY_TARGET = 1e-2
HY_FAST_DECAY = 0.3
HY_SLOW_DECAY = 1.5
HY_MOD_SHIFT = 0.05
N_GROUPS = 4
EXPERTS_PER_GROUP = 8
N_EXPERTS = N_GROUPS * EXPERTS_PER_GROUP
D_EXPERT = 256
EPS = 1e-6
NEG_INF = -1e30

LANES = 128
HEAD_PAD = LANES
MOD_ROWS = 8
VMEM_LIMIT = 56 * 1024 * 1024

MLA_SCALE = QK_DIM ** -0.5
NA_SCALE = NA_HEAD_DIM ** -0.5

NA_QROWS = 8
MOE_TILE = 256
TOK_TILE = 256
DFT_ROWS = 512


def _params(n_axes, vmem=VMEM_LIMIT):
    return pltpu.CompilerParams(dimension_semantics=("arbitrary",) * n_axes, vmem_limit_bytes=vmem)


def _mm(a, b):
    return jnp.dot(a.astype(BF16), b.astype(BF16), preferred_element_type=F32)


def _mm_nt(a, b):
    return lax.dot_general(a.astype(BF16), b.astype(BF16), (((1,), (1,)), ((), ())),
                           preferred_element_type=F32)


def _mm_hi(a, b):
    return jnp.dot(a, b, preferred_element_type=F32, precision=HIGHEST)


def _norm_mod(x, g, sc, sh):
    ms = jnp.mean(x * x, axis=-1, keepdims=True)
    return (x * lax.rsqrt(ms + EPS) * g) * (1.0 + sc) + sh


def _lane_iota(shape):
    return lax.broadcasted_iota(jnp.int32, shape, len(shape) - 1)


class _Layout:
    def __init__(self, batch, seq, dec_b, dec_l, past):
        self.batch, self.seq, self.dec_b, self.dec_l, self.past = batch, seq, dec_b, dec_l, past
        self.n_ctx = batch * seq
        self.n_dec = dec_b * dec_l
        self.n = self.n_ctx + self.n_dec
        self.tm = min(TOK_TILE, seq, dec_l)
        assert self.n_ctx % self.tm == 0 and dec_l % self.tm == 0
        assert self.n_ctx % dec_l == 0 and dec_l % GRID_W == 0
        assert 1 + dec_b <= MOD_ROWS

    def seg(self, i):
        a = self.n_ctx // self.tm
        b = self.dec_l // self.tm
        return jnp.where(i < a, 0, 1 + (i - a) // b)

    def mod_spec(self):
        return pl.BlockSpec((None, 1, D_MODEL), lambda i: (self.seg(i), 0, 0))


def _full(shape):
    nd = len(shape)
    return pl.BlockSpec(shape, lambda *_: (0,) * nd)


def _ada_kernel(ct_ref, w_ref, b_ref, o_ref, *, rows):
    c = ct_ref[...]
    s = c * jax.nn.sigmoid(c)
    w = w_ref[...]
    out = [jnp.sum(s[:, r:r + 1] * w, axis=0, keepdims=True) for r in range(rows)]
    out += [jnp.zeros_like(out[0])] * (MOD_ROWS - rows)
    o_ref[...] = jnp.concatenate(out, axis=0) + b_ref[...]


def _ada_call(cvec_t, ada_w, ada_b, rows):
    depth, d, n6 = ada_w.shape
    tn = 1536
    return pl.pallas_call(
        functools.partial(_ada_kernel, rows=rows),
        grid=(depth, n6 // tn),
        in_specs=[
            pl.BlockSpec((d, MOD_ROWS), lambda l, j: (0, 0)),
            pl.BlockSpec((None, d, tn), lambda l, j: (l, 0, j)),
            pl.BlockSpec((None, 1, tn), lambda l, j: (l, 0, j)),
        ],
        out_specs=pl.BlockSpec((None, MOD_ROWS, tn), lambda l, j: (l, 0, j)),
        out_shape=jax.ShapeDtypeStruct((depth, MOD_ROWS, n6), F32),
        compiler_params=_params(2),
        name="ada_mod",
    )(cvec_t, ada_w, ada_b.reshape(depth, 1, n6))


def _dense_kernel(*refs, has_norm, has_bias, has_resid):
    it = iter(refs)
    x_ref = next(it)
    if has_norm:
        g_ref, sc_ref, sh_ref = next(it), next(it), next(it)
    w_ref = next(it)
    b_ref = next(it) if has_bias else None
    if has_resid:
        r_ref, gate_ref = next(it), next(it)
    o_ref = next(it)
    x = x_ref[...]
    if has_norm:
        x = _norm_mod(x.astype(F32), g_ref[...], sc_ref[...], sh_ref[...])
    y = _mm(x, w_ref[...])
    if has_bias:
        y = y + b_ref[...]
    if has_resid:
        y = r_ref[...] + gate_ref[...] * y
    o_ref[...] = y.astype(o_ref.dtype)


def _dense_call(lay, x, w, *, norm=None, bias=None, resid=None, out_dtype=F32, name):
    n, k = x.shape
    m = w.shape[1]
    tm = lay.tm
    args, specs = [x], [pl.BlockSpec((tm, k), lambda i: (i, 0))]
    if norm is not None:
        g, sc, sh = norm
        args += [g, sc, sh]
        specs += [_full((1, k)), lay.mod_spec(), lay.mod_spec()]
    args.append(w)
    specs.append(_full((k, m)))
    if bias is not None:
        args.append(bias)
        specs.append(_full((1, m)))
    if resid is not None:
        r, gate = resid
        args += [r, gate]
        specs += [pl.BlockSpec((tm, m), lambda i: (i, 0)), lay.mod_spec()]
    return pl.pallas_call(
        functools.partial(_dense_kernel, has_norm=norm is not None, has_bias=bias is not None,
                          has_resid=resid is not None),
        grid=(n // tm,),
        in_specs=specs,
        out_specs=pl.BlockSpec((tm, m), lambda i: (i, 0)),
        out_shape=jax.ShapeDtypeStruct((n, m), out_dtype),
        compiler_params=_params(1),
        name=name,
    )(*args)


def _head_rms(xh, gain):
    ss = jnp.sum(xh * xh, axis=-1, keepdims=True) * (1.0 / QK_DIM)
    return xh * lax.rsqrt(ss + EPS) * gain


def _rope(xh, c, s_lo, s_hi):
    quarter = ROPE_DIM // 4
    return xh * c + pltpu.roll(xh, HEAD_PAD - quarter, 1) * s_lo + pltpu.roll(xh, quarter, 1) * s_hi


def _pair_rms(x, gain):
    lo = _lane_iota(x.shape) < NA_HEAD_DIM
    x2 = x * x
    s_lo = jnp.sum(jnp.where(lo, x2, 0.0), axis=-1, keepdims=True) * (1.0 / NA_HEAD_DIM)
    s_hi = jnp.sum(jnp.where(lo, 0.0, x2), axis=-1, keepdims=True) * (1.0 / NA_HEAD_DIM)
    r = jnp.where(lo, lax.rsqrt(s_lo + EPS), lax.rsqrt(s_hi + EPS))
    return x * r * gain


def _attn_in_kernel(x_ref, g_ref, sc_ref, sh_ref, wql_ref, wkvl_ref, wkr_ref, wna_ref, qn_ref, kvn_ref,
                    wqu_ref, wku_ref, wvu_ref, qg_ref, kg_ref, nqg_ref, nkg_ref, rc_ref, rlo_ref, rhi_ref,
                    q_ref, k_ref, v_ref, nq_ref, nk_ref, nv_ref, ckv_ref, krp_ref, *, ctx_tiles):
    latent = (pl.program_id(0) >= ctx_tiles).astype(F32)
    rc = 1.0 + latent * (rc_ref[...] - 1.0)
    rlo = latent * rlo_ref[...]
    rhi = latent * rhi_ref[...]

    h = _norm_mod(x_ref[...], g_ref[...], sc_ref[...], sh_ref[...]).astype(BF16)
    q_lat = _mm(h, wql_ref[...])
    kv_lat = _mm(h, wkvl_ref[...])
    krp = _mm(h, wkr_ref[...])
    na = _mm(h, wna_ref[...])

    def rms(x, g):
        return x * lax.rsqrt(jnp.mean(x * x, axis=-1, keepdims=True) + EPS) * g

    ckv = rms(kv_lat, kvn_ref[...])
    ckv_ref[...] = ckv
    krp_ref[...] = krp
    q_up = _mm(rms(q_lat, qn_ref[...]), wqu_ref[...])
    k_up = _mm(ckv, wku_ref[...])
    v_ref[...] = _mm(ckv, wvu_ref[...]).astype(v_ref.dtype)
    qg, kg = qg_ref[...], kg_ref[...]
    for hd in range(MLA_HEADS):
        sl = slice(hd * HEAD_PAD, (hd + 1) * HEAD_PAD)
        q_ref[:, sl] = _rope(_head_rms(q_up[:, sl], qg), rc, rlo, rhi).astype(q_ref.dtype)
        k_ref[:, sl] = _rope(_head_rms(k_up[:, sl] + krp, kg), rc, rlo, rhi).astype(k_ref.dtype)
    nqg, nkg = nqg_ref[...], nkg_ref[...]
    for j in range(NA_WIDTH // LANES):
        sl = slice(j * LANES, (j + 1) * LANES)
        nq_ref[:, sl] = _pair_rms(na[:, j * LANES:(j + 1) * LANES], nqg).astype(nq_ref.dtype)
        nk_ref[:, sl] = _pair_rms(na[:, NA_WIDTH + j * LANES:NA_WIDTH + (j + 1) * LANES], nkg)
    nv_ref[...] = na[:, 2 * NA_WIDTH:]


def _attn_in_call(lay, x, norm, w, rope):
    n, d = x.shape
    tm = lay.tm
    ctx_tiles = lay.n_ctx // tm
    dec_tiles = lay.dec_l // tm
    g, sc, sh = norm

    def rope_map(i):
        return (jnp.where(i < ctx_tiles, 0, (i - ctx_tiles) % dec_tiles), 0)

    tok = lambda width: pl.BlockSpec((tm, width), lambda i: (i, 0))
    hp = MLA_HEADS * HEAD_PAD
    in_specs = [tok(d), _full((1, d)), lay.mod_spec(), lay.mod_spec(),
                _full(w["wql"].shape), _full(w["wkvl"].shape), _full(w["wkr"].shape), _full(w["wna"].shape),
                _full((1, Q_RANK)), _full((1, KV_RANK)),
                _full(w["wqu"].shape), _full(w["wku"].shape), _full(w["wvu"].shape),
                _full((1, HEAD_PAD)), _full((1, HEAD_PAD)), _full((1, LANES)), _full((1, LANES)),
                pl.BlockSpec((tm, HEAD_PAD), rope_map), pl.BlockSpec((tm, HEAD_PAD), rope_map),
                pl.BlockSpec((tm, HEAD_PAD), rope_map)]
    out_shapes = [jax.ShapeDtypeStruct((n, hp), BF16), jax.ShapeDtypeStruct((n, hp), BF16),
                  jax.ShapeDtypeStruct((n, MLA_HEADS * V_DIM), BF16), jax.ShapeDtypeStruct((n, NA_WIDTH), BF16),
                  jax.ShapeDtypeStruct((n, NA_WIDTH), F32), jax.ShapeDtypeStruct((n, NA_WIDTH), F32),
                  jax.ShapeDtypeStruct((n, KV_RANK), F32), jax.ShapeDtypeStruct((n, HEAD_PAD), F32)]
    out_specs = [tok(s.shape[1]) for s in out_shapes]
    return pl.pallas_call(
        functools.partial(_attn_in_kernel, ctx_tiles=ctx_tiles),
        grid=(n // tm,),
        in_specs=in_specs,
        out_specs=out_specs,
        out_shape=out_shapes,
        compiler_params=_params(1),
        name="attn_in",
    )(x, g, sc, sh, w["wql"], w["wkvl"], w["wkr"], w["wna"], w["qn"], w["kvn"], w["wqu"], w["wku"], w["wvu"],
      w["qg"], w["kg"], w["nqg"], w["nkg"], *rope)


def _ctx_expand_kernel(ckv_ref, krp_ref, wku_ref, wvu_ref, kg_ref, k_ref, v_ref):
    ckv = ckv_ref[...]
    krp = krp_ref[...]
    k_up = _mm(ckv, wku_ref[...])
    v_ref[...] = _mm(ckv, wvu_ref[...]).astype(v_ref.dtype)
    kg = kg_ref[...]
    for hd in range(MLA_HEADS):
        sl = slice(hd * HEAD_PAD, (hd + 1) * HEAD_PAD)
        k_ref[:, sl] = _head_rms(k_up[:, sl] + krp, kg).astype(k_ref.dtype)


def _ctx_expand_call(ckv, krp, w):
    n = ckv.shape[0]
    tm = min(512, n)
    hp = MLA_HEADS * HEAD_PAD
    return pl.pallas_call(
        _ctx_expand_kernel,
        grid=(n // tm,),
        in_specs=[pl.BlockSpec((tm, KV_RANK), lambda i: (i, 0)), pl.BlockSpec((tm, HEAD_PAD), lambda i: (i, 0)),
                  _full(w["wku"].shape), _full(w["wvu"].shape), _full((1, HEAD_PAD))],
        out_specs=[pl.BlockSpec((tm, hp), lambda i: (i, 0)), pl.BlockSpec((tm, MLA_HEADS * V_DIM), lambda i: (i, 0))],
        out_shape=[jax.ShapeDtypeStruct((n, hp), BF16), jax.ShapeDtypeStruct((n, MLA_HEADS * V_DIM), BF16)],
        compiler_params=_params(1),
        name="ctx_expand",
    )(ckv, krp, w["wku"], w["wvu"], w["kg"])


def _softmax_pv(parts):
    m = functools.reduce(jnp.maximum, [jnp.max(s, axis=-1, keepdims=True) for s, _ in parts])
    acc, den = None, None
    for s, v in parts:
        e = jnp.exp(s - m)
        l = jnp.sum(e, axis=-1, keepdims=True)
        o = _mm(e, v)
        acc = o if acc is None else acc + o
        den = l if den is None else den + l
    return acc / den


def _ctx_attn_kernel(q_ref, k_ref, v_ref, nq_ref, nk_ref, nv_ref, o_ref):
    lo = _lane_iota((q_ref.shape[0], LANES)) < V_DIM
    for j in range(MLA_HEADS // 2):
        vp = v_ref[:, j * LANES:(j + 1) * LANES]
        outs = []
        for hd in (2 * j, 2 * j + 1):
            sl = slice(hd * HEAD_PAD, (hd + 1) * HEAD_PAD)
            s = _mm_nt(q_ref[:, sl], k_ref[:, sl])
            outs.append(_softmax_pv([(s, vp)]))
        o_ref[:, j * LANES:(j + 1) * LANES] = jnp.where(lo, outs[0], outs[1]).astype(o_ref.dtype)
    base = MLA_HEADS * V_DIM
    for j in range(NA_HEADS // 2):
        sl = slice(j * LANES, (j + 1) * LANES)
        qp = nq_ref[:, sl]
        kp = nk_ref[:, sl].astype(BF16)
        vp = nv_ref[:, sl].astype(BF16)
        outs = []
        for half in (0, 1):
            qm = jnp.where(lo if half == 0 else jnp.logical_not(lo), qp, jnp.zeros_like(qp))
            s = _mm_nt(qm, kp)
            outs.append(_softmax_pv([(s, vp)]))
        o_ref[:, base + j * LANES:base + (j + 1) * LANES] = jnp.where(lo, outs[0], outs[1]).astype(o_ref.dtype)


def _ctx_attn_call(lay, q, k, v, nq, nk, nv, out):
    s = lay.seq
    blk = lambda width: pl.BlockSpec((s, width), lambda b: (b, 0))
    return pl.pallas_call(
        _ctx_attn_body,
        grid=(lay.batch,),
        in_specs=[blk(q.shape[1]), blk(k.shape[1]), blk(v.shape[1]), blk(nq.shape[1]), blk(nk.shape[1]),
                  blk(nv.shape[1]), pl.BlockSpec(memory_space=pl.ANY)],
        out_specs=blk(out.shape[1]),
        out_shape=jax.ShapeDtypeStruct(out.shape, out.dtype),
        input_output_aliases={6: 0},
        compiler_params=_params(1),
        name="ctx_attn",
    )(q, k, v, nq, nk, nv, out)


def _ctx_attn_body(q_ref, k_ref, v_ref, nq_ref, nk_ref, nv_ref, alias_ref, o_ref):
    del alias_ref
    _ctx_attn_kernel(q_ref, k_ref, v_ref, nq_ref, nk_ref, nv_ref, o_ref)


def _mla_attn_kernel(q_ref, k_ref, v_ref, kc_ref, vc_ref, alias_ref, o_ref):
    del alias_ref
    lo = _lane_iota((q_ref.shape[0], LANES)) < V_DIM
    for j in range(MLA_HEADS // 2):
        vp = v_ref[:, j * LANES:(j + 1) * LANES]
        vcp = vc_ref[:, j * LANES:(j + 1) * LANES]
        outs = []
        for hd in (2 * j, 2 * j + 1):
            sl = slice(hd * HEAD_PAD, (hd + 1) * HEAD_PAD)
            qh = q_ref[:, sl]
            s1 = _mm_nt(qh, k_ref[:, sl])
            s2 = _mm_nt(qh, kc_ref[:, sl])
            outs.append(_softmax_pv([(s1, vp), (s2, vcp)]))
        o_ref[:, j * LANES:(j + 1) * LANES] = jnp.where(lo, outs[0], outs[1]).astype(o_ref.dtype)


def _mla_attn_call(lay, q, k, v, kc, vc, out):
    L, P = lay.dec_l, lay.past
    tq = min(256, L)
    qb = L // tq
    off_q = lay.n_ctx // tq
    off_l = lay.n_ctx // L
    hp = MLA_HEADS * HEAD_PAD
    vw = MLA_HEADS * V_DIM
    return pl.pallas_call(
        _mla_attn_kernel,
        grid=(lay.dec_b, qb),
        in_specs=[pl.BlockSpec((tq, hp), lambda b, i: (off_q + b * qb + i, 0)),
                  pl.BlockSpec((L, hp), lambda b, i: (off_l + b, 0)),
                  pl.BlockSpec((L, vw), lambda b, i: (off_l + b, 0)),
                  pl.BlockSpec((P, hp), lambda b, i: (b, 0)),
                  pl.BlockSpec((P, vw), lambda b, i: (b, 0)),
                  pl.BlockSpec(memory_space=pl.ANY)],
        out_specs=pl.BlockSpec((tq, vw), lambda b, i: (off_q + b * qb + i, 0)),
        out_shape=jax.ShapeDtypeStruct(out.shape, out.dtype),
        input_output_aliases={5: 0},
        compiler_params=_params(2),
        name="mla_attn",
    )(q, k, v, kc, vc, out)


def _na_bias_kernel(rpb_ref, o_ref):
    h = pl.program_id(0)
    n_dy, n_dx = 2 * NA_KH - 1, 2 * NA_KW - 1
    qc = lax.broadcasted_iota(jnp.int32, (GRID_W, LANES), 0)
    kc = _lane_iota((GRID_W, LANES)) & (GRID_W - 1)
    dx = jnp.clip(kc - qc, -(NA_KW - 1), NA_KW - 1) + NA_KW - 1
    cs = jnp.clip(qc - NA_KW // 2, 0, GRID_W - NA_KW)
    col_ok = (kc >= cs) & (kc < cs + NA_KW)
    tiles = []
    for dy in range(n_dy):
        acc = jnp.zeros((GRID_W, LANES), F32)
        for i in range(n_dx):
            acc = jnp.where(dx == i, rpb_ref[(h * n_dy + dy) * n_dx + i], acc)
        tiles.append(jnp.where(col_ok, acc, NEG_INF))
    zero = jnp.zeros((GRID_W, LANES), F32)
    tiles = [zero] + tiles + [zero]
    lo = _lane_iota((GRID_W, LANES)) < GRID_W
    for i in range(n_dy + 1):
        o_ref[i] = jnp.where(lo, tiles[i], tiles[i + 1])


def _na_bias_call(rpb):
    n_pairs = 2 * NA_KH
    return pl.pallas_call(
        _na_bias_kernel,
        grid=(NA_HEADS,),
        in_specs=[pl.BlockSpec(memory_space=pltpu.SMEM)],
        out_specs=pl.BlockSpec((n_pairs, GRID_W, LANES), lambda h: (h, 0, 0)),
        out_shape=jax.ShapeDtypeStruct((NA_HEADS * n_pairs, GRID_W, LANES), F32),
        compiler_params=_params(1),
        name="na_bias",
    )(rpb.reshape(-1))


def _na_attn_kernel(q_ref, k_ref, v_ref, kc_ref, vc_ref, tp_ref, alias_ref, o_ref, *, rows, win):
    del alias_ref
    g = pl.program_id(1)
    n_pairs = 2 * NA_KH
    nq = NA_QROWS * GRID_W
    nk = win * GRID_W
    ws = jnp.clip(NA_QROWS * g - NA_KH // 2, 0, rows - win)
    start = pl.multiple_of(ws * GRID_W, GRID_W)
    kwin = k_ref[pl.ds(start, nk), :].astype(BF16)
    vwin = v_ref[pl.ds(start, nk), :].astype(BF16)
    kctx = kc_ref[...].astype(BF16)
    vctx = vc_ref[...].astype(BF16)

    qr = NA_QROWS * g + (lax.broadcasted_iota(jnp.int32, (nq, nk), 0) >> GRID_SHIFT)
    kr = ws + (_lane_iota((nq, nk)) >> GRID_SHIFT)
    rs = jnp.clip(qr - NA_KH // 2, 0, rows - NA_KH)
    row_ok = (kr >= rs) & (kr < rs + NA_KH)

    lo = _lane_iota((nq, LANES)) < NA_HEAD_DIM
    for j in range(NA_HEADS // 2):
        sl = slice(j * LANES, (j + 1) * LANES)
        qp = q_ref[:, sl]
        kp, vp, kcp, vcp = kwin[:, sl], vwin[:, sl], kctx[:, sl], vctx[:, sl]
        outs = []
        for half in (0, 1):
            hd = 2 * j + half
            bias_rows = []
            for a in range(NA_QROWS):
                blocks = []
                for p in range(win // 2):
                    dy = ws + 2 * p - (NA_QROWS * g + a) + NA_KH - 1
                    idx = jnp.clip(dy + 1, 0, n_pairs - 1)
                    blocks.append(tp_ref[hd * n_pairs + idx])
                bias_rows.append(jnp.concatenate(blocks, axis=1))
            bias = jnp.concatenate(bias_rows, axis=0)
            qm = jnp.where(lo if half == 0 else jnp.logical_not(lo), qp, jnp.zeros_like(qp))
            s_w = jnp.where(row_ok, _mm_nt(qm, kp) + bias, NEG_INF)
            s_c = _mm_nt(qm, kcp)
            outs.append(_softmax_pv([(s_w, vp), (s_c, vcp)]))
        o_ref[:, sl] = jnp.where(lo, outs[0], outs[1]).astype(o_ref.dtype)


def _na_attn_call(lay, nq, nk, nv, kc, vc, tp, out):
    L, P = lay.dec_l, lay.past
    rows = L // GRID_W
    assert rows % NA_QROWS == 0
    win = min(2 * NA_KH, rows)
    groups = rows // NA_QROWS
    tq = NA_QROWS * GRID_W
    off_q = lay.n_ctx // tq
    off_l = lay.n_ctx // L
    n_out_blk = out.shape[1] // NA_WIDTH
    return pl.pallas_call(
        functools.partial(_na_attn_kernel, rows=rows, win=win),
        grid=(lay.dec_b, groups),
        in_specs=[pl.BlockSpec((tq, NA_WIDTH), lambda b, g: (off_q + b * groups + g, 0)),
                  pl.BlockSpec((L, NA_WIDTH), lambda b, g: (off_l + b, 0)),
                  pl.BlockSpec((L, NA_WIDTH), lambda b, g: (off_l + b, 0)),
                  pl.BlockSpec((P, NA_WIDTH), lambda b, g: (b, 0)),
                  pl.BlockSpec((P, NA_WIDTH), lambda b, g: (b, 0)),
                  _full(tp.shape),
                  pl.BlockSpec(memory_space=pl.ANY)],
        out_specs=pl.BlockSpec((tq, NA_WIDTH), lambda b, g: (off_q + b * groups + g, n_out_blk - 1)),
        out_shape=jax.ShapeDtypeStruct(out.shape, out.dtype),
        input_output_aliases={6: 0},
        compiler_params=_params(2),
        name="na_attn",
    )(nq, nk, nv, kc, vc, tp, out)


def _hy_filter_kernel(z_ref, w1_ref, b1_ref, fr_ref, w2_ref, b2_ref, w3a_ref, w3b_ref, w3c_ref, w3d_ref,
                      dl_ref, t_ref, alt_ref, wf_ref, c_ref, s_ref, kr_ref, ki_ref, kl_ref):
    h = jnp.sin(fr_ref[0:1, :] * (_mm_hi(z_ref[...], w1_ref[...]) + b1_ref[...]))
    h = jnp.sin(fr_ref[1:2, :] * (_mm_hi(h, w2_ref[...]) + b2_ref[...]))
    window = jnp.exp(-t_ref[...] * dl_ref[...]) + HY_MOD_SHIFT
    first = lax.broadcasted_iota(jnp.int32, window.shape, 0) == 0
    alt = alt_ref[...]
    wf = wf_ref[...]
    fwd = (w3a_ref, w3b_ref)
    bwd = (w3c_ref, w3d_ref)
    for o in range(HY_ORDER):
        hf = _mm_hi(h, fwd[o][...]) * window
        hb = jnp.where(first, 0.0, _mm_hi(h, bwd[o][...]) * window)
        norm = jnp.sum(jnp.abs(hf), axis=0, keepdims=True) + jnp.sum(jnp.abs(hb), axis=0, keepdims=True) + EPS
        a = (hf + hb) / norm
        d = (hf - hb) / norm
        kl_ref[o] = jnp.sum(a * alt, axis=0, keepdims=True) * (0.5 / a.shape[0])
        a, d = a.astype(BF16), d.astype(BF16)
        n_rows = a.shape[0]
        ch = min(DFT_ROWS, n_rows)
        for i in range(n_rows // ch):
            rows = slice(i * ch, (i + 1) * ch)
            kr_ref[o, rows, :] = jnp.dot(c_ref[rows, :], a, preferred_element_type=F32) * wf[rows]
            ki_ref[o, rows, :] = -jnp.dot(s_ref[rows, :], d, preferred_element_type=F32) * wf[rows]


def _hy_filter_call(L, z, p, consts):
    d = D_MODEL
    dc = 256
    nc = d // dc
    w3 = p["w3"]
    w3_spec = lambda k: pl.BlockSpec((HY_FF, dc), lambda c: (0, k * nc + c))
    zpad = z.shape[1]
    return pl.pallas_call(
        _hy_filter_kernel,
        grid=(nc,),
        in_specs=[_full((L, zpad)), _full((zpad, HY_FF)), _full((1, HY_FF)), _full((2, HY_FF)),
                  _full((HY_FF, HY_FF)), _full((1, HY_FF)), w3_spec(0), w3_spec(1), w3_spec(2), w3_spec(3),
                  pl.BlockSpec((1, dc), lambda c: (0, c)), _full((L, 1)), _full((L, 1)), _full((L, 1)),
                  _full((L, L)), _full((L, L))],
        out_specs=[pl.BlockSpec((HY_ORDER, L, dc), lambda c: (0, 0, c)),
                   pl.BlockSpec((HY_ORDER, L, dc), lambda c: (0, 0, c)),
                   pl.BlockSpec((HY_ORDER, 1, dc), lambda c: (0, 0, c))],
        out_shape=[jax.ShapeDtypeStruct((HY_ORDER, L, d), F32), jax.ShapeDtypeStruct((HY_ORDER, L, d), F32),
                   jax.ShapeDtypeStruct((HY_ORDER, 1, d), F32)],
        compiler_params=_params(1),
        name="hy_filter",
    )(z, p["w1"], p["b1"], p["freq"], p["w2"], p["b2"], w3, w3, w3, w3,
      consts["deltas"], consts["t"], consts["alt"], consts["wf"], consts["C"], consts["S"])


def _hy_stage_kernel(u_ref, g_ref, cwu_ref, cbu_ref, cwg_ref, cbg_ref, kr_ref, ki_ref, kl_ref, skip_ref, alt_ref,
                     c_ref, s_ref, *out_refs, conv_u):
    o_ref = out_refs[-1]
    L = u_ref.shape[0]
    row = lax.broadcasted_iota(jnp.int32, u_ref.shape, 0)
    first, last = row == 0, row == L - 1

    def short_conv(ref, w_ref, b_ref):
        x = ref[...]
        prev = jnp.where(first, 0.0, pltpu.roll(x, 1, 0))
        nxt = jnp.where(last, 0.0, pltpu.roll(x, L - 1, 0))
        w = w_ref[...]
        return prev * w[0:1] + x * w[1:2] + nxt * w[2:3] + b_ref[...]

    u = short_conv(u_ref, cwu_ref, cbu_ref) if conv_u else u_ref[...]
    gate = short_conv(g_ref, cwg_ref, cbg_ref)
    alt = alt_ref[...]
    ub = u.astype(BF16)
    ul_kl = jnp.sum(u * alt, axis=0, keepdims=True) * kl_ref[...]
    skip = skip_ref[...]
    ch = min(DFT_ROWS, L)
    yr, zi = [], []
    for i in range(L // ch):
        rows = slice(i * ch, (i + 1) * ch)
        ur = jnp.dot(c_ref[rows, :], ub, preferred_element_type=F32)
        us = jnp.dot(s_ref[rows, :], ub, preferred_element_type=F32)
        kr, ki = kr_ref[rows, :], ki_ref[rows, :]
        yr.append((ur * kr + us * ki).astype(BF16))
        zi.append((us * kr - ur * ki).astype(BF16))
    yr = jnp.concatenate(yr, axis=0)
    zi = jnp.concatenate(zi, axis=0)
    for i in range(L // ch):
        rows = slice(i * ch, (i + 1) * ch)
        y = jnp.dot(c_ref[rows, :], yr, preferred_element_type=F32) + jnp.dot(s_ref[rows, :], zi,
                                                                              preferred_element_type=F32)
        y = y + alt[rows] * ul_kl + u[rows] * skip
        o_ref[rows, :] = (gate[rows] * y).astype(o_ref.dtype)


def _hy_stage_call(u, u_blk, gate, g_blk, cwu, cbu, cwg, cbg, kr, ki, kl, skip, consts, *, conv_u, L, nseq,
                   row_off, out, td):
    d = D_MODEL
    nc = d // td
    off = row_off // L
    one = pl.Buffered(1)
    chan = lambda rows: pl.BlockSpec((rows, td), lambda c, b: (0, c))
    args = [u, gate, cwu, cbu, cwg, cbg, kr, ki, kl, skip, consts["alt"], consts["C"], consts["S"]]
    in_specs = [pl.BlockSpec((L, td), lambda c, b: (off + b, u_blk * nc + c)),
                pl.BlockSpec((L, td), lambda c, b: (off + b, g_blk * nc + c)),
                chan(3), chan(1), chan(3), chan(1),
                pl.BlockSpec((L, td), lambda c, b: (0, c), pipeline_mode=one),
                pl.BlockSpec((L, td), lambda c, b: (0, c), pipeline_mode=one),
                chan(1), chan(1),
                _full((L, 1)),
                pl.BlockSpec((L, L), lambda c, b: (0, 0), pipeline_mode=one),
                pl.BlockSpec((L, L), lambda c, b: (0, 0), pipeline_mode=one)]
    aliases = {}
    if not isinstance(out, jax.ShapeDtypeStruct):
        aliases = {len(args): 0}
        args.append(out)
        in_specs.append(pl.BlockSpec(memory_space=pl.ANY))
    return pl.pallas_call(
        functools.partial(_hy_stage_kernel, conv_u=conv_u),
        grid=(nc, nseq),
        in_specs=in_specs,
        out_specs=pl.BlockSpec((L, td), lambda c, b: (off + b, c)),
        out_shape=jax.ShapeDtypeStruct(out.shape, out.dtype),
        input_output_aliases=aliases,
        compiler_params=_params(2),
        name="hy_stage_%d" % L,
    )(*args)


def _dft_consts(L):
    f = jnp.arange(L, dtype=jnp.int32)
    blk = min(256, L)
    ang = lambda rows: ((rows[:, None] * f[None, :]) % (2 * L)).astype(F32) * (math.pi / L)
    base, seed = ang(f[:blk]), ang(f[::blk])
    bc, bs, sc_, ss_ = jnp.cos(base)[None], jnp.sin(base)[None], jnp.cos(seed)[:, None], jnp.sin(seed)[:, None]
    cmat = (bc * sc_ - bs * ss_).reshape(L, L)
    smat = (bs * sc_ + bc * ss_).reshape(L, L)
    t = jnp.linspace(0.0, 1.0, L, dtype=F32)[:, None]
    w = 2.0 * math.pi * jnp.arange(L, dtype=F32)[:, None] / L
    bands = jnp.linspace(1e-4, HY_BANDS - 1, HY_BANDS, dtype=F32)[None, :]
    z = jnp.concatenate([t, jnp.cos(bands * w), -jnp.sin(bands * w)], axis=-1)
    z = jnp.pad(z, ((0, 0), (0, 32 - HY_EMB)))
    deltas = jnp.abs(jnp.linspace(math.log(HY_TARGET) / HY_FAST_DECAY, math.log(HY_TARGET) / HY_SLOW_DECAY,
                                  D_MODEL, dtype=F32))[None, :]
    alt = (1.0 - 2.0 * (f % 2).astype(F32))[:, None]
    wf = jnp.where(f == 0, 0.5 / L, 1.0 / L).astype(F32)[:, None]
    return {"C": cmat.astype(BF16), "S": smat.astype(BF16), "t": t, "z": z, "deltas": deltas,
            "alt": alt, "wf": wf}


def _route_kernel(x_ref, g_ref, sc_ref, sh_ref, wr_ref, br_ref, h_ref, meta_ref, cnt_ref, base_ref):
    i = pl.program_id(0)
    tm = x_ref.shape[0]

    @pl.when(i == 0)
    def _():
        base_ref[...] = jnp.zeros_like(base_ref)

    h = _norm_mod(x_ref[...], g_ref[...], sc_ref[...], sh_ref[...])
    h_ref[...] = h
    logits = _mm_hi(h, wr_ref[...]) + br_ref[...]
    lane_i = _lane_iota(logits.shape)
    lane = lane_i.astype(F32)
    grp_of_lane = (lane_i >> 3).astype(F32)
    big = 1e6
    is_g = (lane_i >= N_EXPERTS) & (lane_i < N_EXPERTS + N_GROUPS)
    glog = jnp.where(is_g, logits, NEG_INF)
    gmax = jnp.max(glog, axis=-1, keepdims=True)
    g_w = 1.0 / jnp.sum(jnp.exp(glog - gmax), axis=-1, keepdims=True)
    g_i = jnp.min(jnp.where(glog == gmax, lane - N_EXPERTS, big), axis=-1, keepdims=True)
    in_grp = (lane_i < N_EXPERTS) & (grp_of_lane == g_i)
    elog = jnp.where(in_grp, logits, NEG_INF)
    m1 = jnp.max(elog, axis=-1, keepdims=True)
    i1 = jnp.min(jnp.where(elog == m1, lane, big), axis=-1, keepdims=True)
    elog2 = jnp.where(lane == i1, NEG_INF, elog)
    m2 = jnp.max(elog2, axis=-1, keepdims=True)
    i2 = jnp.min(jnp.where(elog2 == m2, lane, big), axis=-1, keepdims=True)
    a2 = jnp.exp(m2 - m1)
    w1 = g_w / (1.0 + a2)
    w2 = g_w * a2 / (1.0 + a2)

    oh1 = jnp.where(lane == i1, 1.0, 0.0).astype(BF16)
    oh2 = jnp.where(lane == i2, 1.0, 0.0).astype(BF16)
    tri = jnp.where(lax.broadcasted_iota(jnp.int32, (tm, tm), 0) > lax.broadcasted_iota(jnp.int32, (tm, tm), 1),
                    1.0, 0.0).astype(BF16)
    cum1 = jnp.dot(tri, oh1, preferred_element_type=F32)
    cum2 = jnp.dot(tri, oh2, preferred_element_type=F32)
    tot1 = jnp.sum(oh1.astype(F32), axis=0, keepdims=True)
    tot2 = jnp.sum(oh2.astype(F32), axis=0, keepdims=True)
    base = base_ref[...]
    r1 = jnp.sum(jnp.where(lane == i1, base + cum1, 0.0), axis=-1, keepdims=True)
    r2 = jnp.sum(jnp.where(lane == i2, base + tot1 + cum2, 0.0), axis=-1, keepdims=True)
    base = base + tot1 + tot2
    base_ref[...] = base
    cnt_ref[...] = base
    cols = [i1, i2, r1, r2, w1, w2]
    meta = jnp.zeros(logits.shape, F32)
    for c, val in enumerate(cols):
        meta = jnp.where(lane_i == c, val, meta)
    meta_ref[...] = meta


def _route_call(lay, x, norm, wr, br):
    n, d = x.shape
    tm = lay.tm
    g, sc, sh = norm
    tok = lambda width: pl.BlockSpec((tm, width), lambda i: (i, 0))
    return pl.pallas_call(
        _route_kernel,
        grid=(n // tm,),
        in_specs=[tok(d), _full((1, d)), lay.mod_spec(), lay.mod_spec(), _full((d, LANES)), _full((1, LANES))],
        out_specs=[tok(d), tok(LANES), _full((1, LANES))],
        out_shape=[jax.ShapeDtypeStruct((n, d), F32), jax.ShapeDtypeStruct((n, LANES), F32),
                   jax.ShapeDtypeStruct((1, LANES), F32)],
        scratch_shapes=[pltpu.VMEM((1, LANES), F32)],
        compiler_params=_params(1),
        name="moe_route",
    )(x, g, sc, sh, wr, br)


def _row_copy(src_ref, s, dst_ref, d, sem):
    return pltpu.make_async_copy(src_ref.at[pl.ds(s, 1)], dst_ref.at[pl.ds(d, 1)], sem)


def _expert_kernel(te_ref, src_ref, nxt_ref, dst_ref, h_ref, wg_ref, wu_ref, wd_ref, o_ref,
                   xbuf, ybuf, zbuf, sem_in, sem_out, sem_z, *, trash_base):
    j = pl.program_id(0)
    tiles = pl.num_programs(0)
    used = te_ref[tiles]
    slot = lax.rem(j, 2)
    tm = xbuf.shape[1]

    def start_gather(idx_ref, s):
        for r in range(tm):
            _row_copy(h_ref, idx_ref[0, r], xbuf.at[s], r, sem_in.at[s]).start(priority=r % 2)

    def wait_gather(s):
        for r in range(tm):
            _row_copy(h_ref, 0, xbuf.at[s], r, sem_in.at[s]).wait()

    def start_scatter(s):
        for r in range(tm):
            _row_copy(ybuf.at[s], r, o_ref, dst_ref[0, r], sem_out.at[s]).start(priority=r % 2)

    def wait_scatter(s):
        for r in range(tm):
            _row_copy(ybuf.at[s], r, o_ref, 0, sem_out.at[s]).wait()

    @pl.when(j == 0)
    def _():
        zbuf[...] = jnp.zeros_like(zbuf)
        fills = [pltpu.make_async_copy(zbuf, o_ref.at[pl.ds(trash_base + k * tm, tm)], sem_z)
                 for k in range(N_EXPERTS)]
        for f in fills:
            f.start()
        start_gather(src_ref, 0)
        for f in fills:
            f.wait()

    @pl.when(j + 1 < used)
    def _():
        start_gather(nxt_ref, 1 - slot)

    @pl.when(j < used)
    def _():
        wait_gather(slot)

        @pl.when(j >= 2)
        def _():
            wait_scatter(slot)

        x = xbuf[slot].astype(BF16)
        hg = _mm(x, wg_ref[...])
        hu = _mm(x, wu_ref[...])
        act = hg * jax.nn.sigmoid(hg) * hu
        ybuf[slot] = _mm(act, wd_ref[...])
        start_scatter(slot)

        @pl.when(j == used - 1)
        def _():
            wait_scatter(slot)

            @pl.when(j >= 1)
            def _():
                wait_scatter(1 - slot)


def _expert_call(layer, tile_expert, src, dst, h, wg, wu, wd):
    n, d = h.shape
    f = wg.shape[-1]
    tiles = src.shape[0]
    tm = MOE_TILE
    idx = lambda fn: pl.BlockSpec((None, 1, tm), fn, memory_space=pltpu.SMEM)
    wspec = lambda a, b: pl.BlockSpec((None, None, a, b), lambda j, te: (layer, te[j], 0, 0))
    grid_spec = pltpu.PrefetchScalarGridSpec(
        num_scalar_prefetch=1,
        grid=(tiles,),
        in_specs=[idx(lambda j, te: (j, 0, 0)),
                  idx(lambda j, te: (jnp.minimum(j + 1, tiles - 1), 0, 0)),
                  idx(lambda j, te: (j, 0, 0)),
                  pl.BlockSpec(memory_space=pl.ANY),
                  wspec(d, f), wspec(d, f), wspec(f, d)],
        out_specs=pl.BlockSpec(memory_space=pl.ANY),
        scratch_shapes=[pltpu.VMEM((2, tm, d), F32), pltpu.VMEM((2, tm, d), F32), pltpu.VMEM((tm, d), F32),
                        pltpu.SemaphoreType.DMA((2,)), pltpu.SemaphoreType.DMA((2,)), pltpu.SemaphoreType.DMA(())],
    )
    return pl.pallas_call(
        functools.partial(_expert_kernel, trash_base=2 * n),
        grid_spec=grid_spec,
        out_shape=jax.ShapeDtypeStruct((2 * n + N_EXPERTS * tm, d), F32),
        compiler_params=_params(1),
        name="moe_experts",
    )(tile_expert, src, src, dst, h, wg, wu, wd)


def _plan_kernel(pos_ref, fill_ref, inv_ref, sem):
    i = pl.program_id(0)
    ch = pos_ref.shape[1]

    @pl.when(i == 0)
    def _():
        clear = pltpu.make_async_copy(fill_ref, inv_ref, sem)
        clear.start()
        clear.wait()

    def place(t, c):
        inv_ref[pos_ref[0, t]] = i * ch + t
        return c

    lax.fori_loop(0, ch, place, 0, unroll=16)


def _plan_call(pos, rows):
    total = pos.shape[0]
    ch = min(4096, total)
    assert total % ch == 0
    return pl.pallas_call(
        _plan_kernel,
        grid=(total // ch,),
        in_specs=[pl.BlockSpec((None, 1, ch), lambda i: (i, 0, 0), memory_space=pltpu.SMEM),
                  pl.BlockSpec(memory_space=pl.ANY)],
        out_specs=pl.BlockSpec(memory_space=pltpu.SMEM),
        out_shape=jax.ShapeDtypeStruct((rows,), jnp.int32),
        scratch_shapes=[pltpu.SemaphoreType.DMA(())],
        compiler_params=_params(1),
        name="moe_plan",
    )(pos.reshape(total // ch, 1, ch), jnp.full((rows,), -1, jnp.int32))


def _combine_kernel(x_ref, gate_ref, meta_ref, y0_ref, y1_ref, o_ref):
    meta = meta_ref[...]
    o_ref[...] = x_ref[...] + gate_ref[...] * (meta[:, 4:5] * y0_ref[...] + meta[:, 5:6] * y1_ref[...])


def _combine_call(lay, x, gate, meta, ys):
    n, d = x.shape
    tm = lay.tm
    tok = lambda width: pl.BlockSpec((tm, width), lambda i: (i, 0))
    return pl.pallas_call(
        _combine_kernel,
        grid=(n // tm,),
        in_specs=[tok(d), lay.mod_spec(), tok(LANES), tok(d), pl.BlockSpec((tm, d), lambda i: (n // tm + i, 0))],
        out_specs=tok(d),
        out_shape=jax.ShapeDtypeStruct((n, d), F32),
        compiler_params=_params(1),
        name="moe_combine",
    )(x, gate, meta, ys, ys)


def _moe_layer(lay, layer, x, norm, gate, wr, br, wg, wu, wd):
    n = x.shape[0]
    tm = MOE_TILE
    h, meta, counts = _route_call(lay, x, norm, wr, br)
    cnt = counts[0, :N_EXPERTS].astype(jnp.int32)
    padded = ((cnt + tm - 1) // tm) * tm
    ends = jnp.cumsum(padded)
    starts = ends - padded
    m4 = meta[:, 0:4].T.astype(jnp.int32)
    onehot = m4[0:2, :, None] == jnp.arange(N_EXPERTS, dtype=jnp.int32)[None, None, :]
    pos = jnp.sum(jnp.where(onehot, starts[None, None, :], 0), axis=-1) + m4[2:4]
    rows = 2 * n + N_EXPERTS * tm
    tiles = rows // tm
    inv = _plan_call(pos.reshape(-1), rows)
    is_pad = inv < 0
    trash = 2 * n - 1 + jnp.cumsum(is_pad.astype(jnp.int32))
    dst = jnp.where(is_pad, trash, inv).reshape(tiles, 1, tm)
    src = jnp.where(is_pad, 0, jnp.where(inv >= n, inv - n, inv)).reshape(tiles, 1, tm)
    tile_start = jnp.arange(tiles, dtype=jnp.int32) * tm
    tile_expert = jnp.sum((tile_start[:, None] >= ends[None, :]).astype(jnp.int32), axis=1)
    used = ends[-1] // tm
    last = jnp.take(tile_expert, jnp.maximum(used - 1, 0))
    tile_expert = jnp.where(tile_start // tm < used, tile_expert, last)
    tile_expert = jnp.concatenate([tile_expert, used[None]]).astype(jnp.int32)
    ys = _expert_call(layer, tile_expert, src, dst, h, wg, wu, wd)
    return _combine_call(lay, x, gate, meta, ys)


def _attn_weights(a, attn_w_in, mla_q_norm, mla_kv_norm, mla_w_q_up, mla_w_kv_up, mla_q_gain, mla_k_gain,
                  na_q_gain, na_k_gain):
    w_in = attn_w_in[a]
    o1, o2, o3 = Q_RANK, Q_RANK + KV_RANK, Q_RANK + KV_RANK + ROPE_DIM
    pad_head = lambda g: jnp.pad(g, (0, HEAD_PAD - QK_DIM))[None, :]
    wkr = jnp.zeros((D_MODEL, HEAD_PAD), F32).at[:, NOPE_DIM:QK_DIM].set(w_in[:, o2:o3])
    wqu = jnp.pad(mla_w_q_up[a].reshape(Q_RANK, MLA_HEADS, QK_DIM), ((0, 0), (0, 0), (0, HEAD_PAD - QK_DIM)))
    wkv = mla_w_kv_up[a].reshape(KV_RANK, MLA_HEADS, NOPE_DIM + V_DIM)
    wku = jnp.pad(wkv[:, :, :NOPE_DIM], ((0, 0), (0, 0), (0, HEAD_PAD - NOPE_DIM)))
    return {
        "wql": w_in[:, :o1].astype(BF16), "wkvl": w_in[:, o1:o2].astype(BF16), "wkr": wkr.astype(BF16),
        "wna": w_in[:, o3:].astype(BF16),
        "qn": mla_q_norm[a][None, :], "kvn": mla_kv_norm[a][None, :],
        "wqu": wqu.reshape(Q_RANK, MLA_HEADS * HEAD_PAD).astype(BF16),
        "wku": wku.reshape(KV_RANK, MLA_HEADS * HEAD_PAD).astype(BF16),
        "wvu": wkv[:, :, NOPE_DIM:].reshape(KV_RANK, MLA_HEADS * V_DIM).astype(BF16),
        "qg": pad_head(mla_q_gain[a]) * MLA_SCALE, "kg": pad_head(mla_k_gain[a]),
        "nqg": jnp.tile(na_q_gain[a], LANES // NA_HEAD_DIM)[None, :] * NA_SCALE,
        "nkg": jnp.tile(na_k_gain[a], LANES // NA_HEAD_DIM)[None, :],
    }


def _rope_tables(L):
    t = jnp.arange(L)
    quarter = ROPE_DIM // 4
    inv = ROPE_THETA ** (-jnp.arange(quarter, dtype=F32) / quarter)
    ang_r = (t // GRID_W).astype(F32)[:, None] * inv[None, :]
    ang_c = (t % GRID_W).astype(F32)[:, None] * inv[None, :]
    zeros = jnp.zeros((L, quarter), F32)
    ones_nope = jnp.ones((L, NOPE_DIM), F32)
    pad = jnp.zeros((L, HEAD_PAD - QK_DIM), F32)
    cos = jnp.concatenate([ones_nope, jnp.cos(ang_r), jnp.cos(ang_r), jnp.cos(ang_c), jnp.cos(ang_c), pad], axis=1)
    s_lo = jnp.concatenate([0 * ones_nope, -jnp.sin(ang_r), zeros, -jnp.sin(ang_c), zeros, pad], axis=1)
    s_hi = jnp.concatenate([0 * ones_nope, zeros, jnp.sin(ang_r), zeros, jnp.sin(ang_c), pad], axis=1)
    return cos, s_lo, s_hi


def kernel(x_prompt, x_sample, cache_mla_ckv, cache_mla_krope, cache_na_k, cache_na_v, c, c_ctx, ada_w, ada_b, norm_mix, norm_ffn, attn_w_in, mla_q_norm, mla_kv_norm, mla_w_q_up, mla_w_kv_up, mla_q_gain, mla_k_gain, na_q_gain, na_k_gain, na_rpb, attn_w_out, hy_w_in, hy_b_in, hy_conv_w, hy_conv_b, hy_filt_w1, hy_filt_b1, hy_filt_freq, hy_filt_w2, hy_filt_b2, hy_filt_w3, hy_skip, hy_w_out, moe_w_group, moe_b_group, moe_w_router, moe_b_router, moe_w_gate, moe_w_up, moe_w_down):
    batch, seq, d = x_prompt.shape
    dec_b, dec_l, _ = x_sample.shape
    past = cache_mla_ckv.shape[2]
    depth = ada_w.shape[0]
    lay = _Layout(batch, seq, dec_b, dec_l, past)
    n_ctx = lay.n_ctx

    x = jnp.concatenate([x_prompt.reshape(n_ctx, d), x_sample.reshape(lay.n_dec, d)], axis=0)
    cvec = jnp.concatenate([c_ctx[None, :], c], axis=0)
    cvec_t = jnp.pad(cvec, ((0, MOD_ROWS - cvec.shape[0]), (0, 0))).T
    mods = _ada_call(cvec_t, ada_w, ada_b, 1 + dec_b)

    def mod(l, j):
        return mods[l, :, j * d:(j + 1) * d].reshape(MOD_ROWS, 1, d)

    rope = _rope_tables(dec_l)
    consts = {L: _dft_consts(L) for L in sorted({seq, dec_l})}
    ckv_out, krope_out, nk_out, nv_out = [], [], [], []

    for l in range(depth):
        sh1, sc1, g1, sh2, sc2, g2 = [mod(l, j) for j in range(6)]
        norm1 = (norm_mix[l][None, :], sc1, sh1)
        if l % 2 == 0:
            a = l // 2
            w = _attn_weights(a, attn_w_in, mla_q_norm, mla_kv_norm, mla_w_q_up, mla_w_kv_up, mla_q_gain,
                              mla_k_gain, na_q_gain, na_k_gain)
            q, k, v, nq, nk, nv, ckv, krp = _attn_in_call(lay, x, norm1, w, rope)
            ckv_out.append(ckv[:n_ctx].reshape(batch, seq, KV_RANK))
            krope_out.append(krp[:n_ctx, NOPE_DIM:QK_DIM].reshape(batch, seq, ROPE_DIM))
            nk_out.append(nk[:n_ctx].reshape(batch, seq, NA_HEADS, NA_HEAD_DIM))
            nv_out.append(nv[:n_ctx].reshape(batch, seq, NA_HEADS, NA_HEAD_DIM))
            krp_c = jnp.pad(cache_mla_krope[:, a].reshape(dec_b * past, ROPE_DIM),
                            ((0, 0), (NOPE_DIM, HEAD_PAD - QK_DIM)))
            kc, vc = _ctx_expand_call(cache_mla_ckv[:, a].reshape(dec_b * past, KV_RANK), krp_c, w)
            tp = _na_bias_call(na_rpb[a])
            att = jnp.zeros((lay.n, MLA_HEADS * V_DIM + NA_WIDTH), BF16)
            att = _ctx_attn_call(lay, q, k, v, nq, nk, nv, att)
            att = _mla_attn_call(lay, q, k, v, kc, vc, att)
            att = _na_attn_call(lay, nq, nk, nv, cache_na_k[:, a].reshape(dec_b * past, NA_WIDTH),
                                cache_na_v[:, a].reshape(dec_b * past, NA_WIDTH), tp, att)
            x = _dense_call(lay, att, attn_w_out[a].astype(BF16), resid=(x, g1), name="attn_out")
        else:
            j = l // 2
            p3 = _dense_call(lay, x, hy_w_in[j].astype(BF16), norm=norm1, bias=hy_b_in[j][None, :], name="hy_in")
            cw = hy_conv_w[j].reshape(3, 3, d).transpose(1, 0, 2)
            cb = hy_conv_b[j].reshape(3, 1, d)
            skip = hy_skip[j].reshape(HY_ORDER, 1, d)
            filt = {"w1": jnp.pad(hy_filt_w1[j], ((0, 32 - HY_EMB), (0, 0))), "b1": hy_filt_b1[j][None, :],
                    "freq": hy_filt_freq[j], "w2": hy_filt_w2[j], "b2": hy_filt_b2[j][None, :],
                    "w3": hy_filt_w3[j]}
            z1 = jnp.zeros((lay.n, d), F32)
            z = jnp.zeros((lay.n, d), BF16)
            for L, nseq, off, td in ((seq, batch, 0, d), (dec_l, dec_b, n_ctx, 256)):
                kr, ki, kl = _hy_filter_call(L, consts[L]["z"], filt, consts[L])
                geom = dict(L=L, nseq=nseq, row_off=off, td=min(td, d))
                z1 = _hy_stage_call(p3, 0, p3, 1, cw[0], cb[0], cw[1], cb[1], kr[0], ki[0], kl[0], skip[0],
                                    consts[L], conv_u=True, out=z1, **geom)
                z = _hy_stage_call(z1, 0, p3, 2, cw[0], cb[0], cw[2], cb[2], kr[1], ki[1], kl[1], skip[1],
                                   consts[L], conv_u=False, out=z, **geom)
            x = _dense_call(lay, z, hy_w_out[j].astype(BF16), resid=(x, g1), name="hy_out")
        wr = jnp.zeros((d, LANES), F32).at[:, :N_EXPERTS].set(moe_w_router[l])
        wr = wr.at[:, N_EXPERTS:N_EXPERTS + N_GROUPS].set(moe_w_group[l])
        br = jnp.zeros((1, LANES), F32).at[0, :N_EXPERTS].set(moe_b_router[l])
        br = br.at[0, N_EXPERTS:N_EXPERTS + N_GROUPS].set(moe_b_group[l])
        x = _moe_layer(lay, l, x, (norm_ffn[l][None, :], sc2, sh2), g2, wr, br, moe_w_gate, moe_w_up, moe_w_down)

    y_prompt = x[:n_ctx].reshape(batch, seq, d)
    y_sample = x[n_ctx:].reshape(dec_b, dec_l, d)
    return (y_prompt, y_sample, jnp.stack(ckv_out, axis=1), jnp.stack(krope_out, axis=1),
            jnp.stack(nk_out, axis=1), jnp.stack(nv_out, axis=1))
```
